```python
import math
import jax, jax.numpy as jnp
from jax import lax
import numpy as np

D_MODEL = 1024
BATCH = 2
SEQ = 16384
DEPTH = 4

N_ATTN_HEADS = 8
HEAD_DIM = D_MODEL // 16
ATTN_DIM = N_ATTN_HEADS * HEAD_DIM
Q_BLOCK = 128
FORGET_BIAS_MEAN = 3.0
POOL_WINDOWS = (2, 4, 8, 16)
N_POOL_GROUPS = len(POOL_WINDOWS)
POOL_GROUP_DIM = D_MODEL // 8
POOL_DIM = N_POOL_GROUPS * POOL_GROUP_DIM
EVEN_IN_DIM = 3 * ATTN_DIM + N_ATTN_HEADS + POOL_DIM
EVEN_MIX_DIM = ATTN_DIM + POOL_DIM
CONV_DIM = D_MODEL
CONV_WIDTH = 31
D_FF = 2816
FFN_CONV_WIDTH = 3
EPS = 1e-6

N_EVEN = (DEPTH + 1) // 2
N_ODD = DEPTH // 2

kernel_name = "fox_pool_conformer_convffn_hybrid"


def rms_norm(x, g):
    xf = x.astype(jnp.float32)
    y = xf * lax.rsqrt(jnp.mean(xf * xf, axis=-1, keepdims=True) + EPS)
    return (y * g.astype(jnp.float32)).astype(x.dtype)


def layer_norm(x, g, b):
    xf = x.astype(jnp.float32)
    mu = jnp.mean(xf, axis=-1, keepdims=True)
    xc = xf - mu
    y = xc * lax.rsqrt(jnp.mean(xc * xc, axis=-1, keepdims=True) + EPS)
    return (y * g.astype(jnp.float32) + b.astype(jnp.float32)).astype(x.dtype)


def causal_dwconv(x, w, b):
    k_width, ch = w.shape
    y = lax.conv_general_dilated(
        x, w[:, None, :].astype(x.dtype), window_strides=(1,), padding=[(k_width - 1, 0)],
        dimension_numbers=("NWC", "WIO", "NWC"), feature_group_count=ch)
    return y + b.astype(x.dtype)


def forgetting_attention(q, k, v, f_logit, q_g, k_g):
    bsz, seq, nh, dh = q.shape
    nb = seq // Q_BLOCK
    q = rms_norm(q, q_g)
    k = rms_norm(k, k_g)
    scale = 1.0 / math.sqrt(dh)
    c = jnp.cumsum(jax.nn.log_sigmoid(f_logit.astype(jnp.float32)), axis=1)
    c_k = jnp.transpose(c, (0, 2, 1))
    k_h = jnp.transpose(k, (0, 2, 1, 3))
    v_h = jnp.transpose(v, (0, 2, 1, 3))
    q_blocks = q.reshape(bsz, nb, Q_BLOCK, nh, dh).transpose(1, 0, 3, 2, 4)
    c_blocks = c.reshape(bsz, nb, Q_BLOCK, nh).transpose(1, 0, 3, 2)
    k_pos = jnp.arange(seq)

    def one_block(args):
        q_blk, cq_blk, blk = args
        q_pos = blk * Q_BLOCK + jnp.arange(Q_BLOCK)
        s = jnp.einsum("bhqd,bhkd->bhqk", q_blk, k_h).astype(jnp.float32) * scale
        s = s + cq_blk[..., :, None] - c_k[:, :, None, :]
        mask = k_pos[None, :] <= q_pos[:, None]
        s = jnp.where(mask[None, None], s, -jnp.inf)
        p = jax.nn.softmax(s, axis=-1)
        return jnp.einsum("bhqk,bhkd->bhqd", p.astype(v_h.dtype), v_h)

    out = lax.map(one_block, (q_blocks, c_blocks, jnp.arange(nb)))
    return out.transpose(1, 0, 3, 2, 4).reshape(bsz, seq, nh * dh)


def multiscale_pool(u, w_pool, pool_scale):
    bsz, seq, _ = u.shape
    ug = u.reshape(bsz, seq, N_POOL_GROUPS, POOL_GROUP_DIM)
    csz = jnp.pad(jnp.cumsum(ug.astype(jnp.float32), axis=1), ((0, 0), (1, 0), (0, 0), (0, 0)))
    pos = jnp.arange(1, seq + 1, dtype=jnp.float32)
    outs = []
    for g, w in enumerate(POOL_WINDOWS):
        upper = csz[:, 1:, g]
        lower = jnp.pad(csz[:, : seq + 1 - w, g], ((0, 0), (w - 1, 0), (0, 0)))
        count = jnp.minimum(pos, float(w))[None, :, None]
        outs.append((upper - lower) / count)
    pooled = jnp.stack(outs, axis=2).astype(u.dtype)
    mixed = pooled - ug
    y = jnp.einsum("bsgc,gcd->bsgd", mixed, w_pool).reshape(bsz, seq, POOL_DIM)
    return y * pool_scale


def even_mixer(h, w_in, b_f, q_g, k_g, w_pool, pool_scale, w_out):
    bsz, seq, _ = h.shape
    z = h @ w_in
    o = 0
    q = z[..., o:o + ATTN_DIM].reshape(bsz, seq, N_ATTN_HEADS, HEAD_DIM); o += ATTN_DIM
    k = z[..., o:o + ATTN_DIM].reshape(bsz, seq, N_ATTN_HEADS, HEAD_DIM); o += ATTN_DIM
    v = z[..., o:o + ATTN_DIM].reshape(bsz, seq, N_ATTN_HEADS, HEAD_DIM); o += ATTN_DIM
    f_logit = z[..., o:o + N_ATTN_HEADS] + b_f; o += N_ATTN_HEADS
    u = z[..., o:o + POOL_DIM]
    a_out = forgetting_attention(q, k, v, f_logit, q_g, k_g)
    p_out = multiscale_pool(u, w_pool, pool_scale)
    return jnp.concatenate([a_out, p_out], axis=-1) @ w_out


def conformer_conv(h, w_pw1, dw_w, dw_b, ln_g, ln_b, w_pw2):
    a, g = jnp.split(h @ w_pw1, 2, axis=-1)
    u = a * jax.nn.sigmoid(g)
    u = causal_dwconv(u, dw_w, dw_b)
    u = jax.nn.silu(layer_norm(u, ln_g, ln_b))
    return u @ w_pw2


def conv_ffn(h, w_up, conv_w, conv_b, w_down):
    u = causal_dwconv(h @ w_up, conv_w, conv_b)
    gate, val = jnp.split(u, 2, axis=-1)
    return (jax.nn.silu(gate) * val) @ w_down


def setup_inputs(seed: int = 0) -> dict:
    key = jax.random.key(seed)
    ks = jax.random.split(key, 24)
    nrm = lambda k, shape, s: jax.random.normal(k, shape, jnp.float32) * s
    d = D_MODEL
    return {
        "x": nrm(ks[0], (BATCH, SEQ, d), 1.0),
        "even_norm_g": 1.0 + nrm(ks[1], (N_EVEN, d), 0.05),
        "even_w_in": nrm(ks[2], (N_EVEN, d, EVEN_IN_DIM), d ** -0.5),
        "even_b_f": FORGET_BIAS_MEAN + nrm(ks[3], (N_EVEN, N_ATTN_HEADS), 0.5),
        "even_q_norm_g": 1.0 + nrm(ks[4], (N_EVEN, HEAD_DIM), 0.05),
        "even_k_norm_g": 1.0 + nrm(ks[5], (N_EVEN, HEAD_DIM), 0.05),
        "even_w_pool": nrm(ks[6], (N_EVEN, N_POOL_GROUPS, POOL_GROUP_DIM, POOL_GROUP_DIM), POOL_GROUP_DIM ** -0.5),
        "even_pool_scale": 1.0 + nrm(ks[7], (N_EVEN, POOL_DIM), 0.1),
        "even_w_out": nrm(ks[8], (N_EVEN, EVEN_MIX_DIM, d), EVEN_MIX_DIM ** -0.5),
        "odd_norm_g": 1.0 + nrm(ks[9], (N_ODD, d), 0.05),
        "odd_w_pw1": nrm(ks[10], (N_ODD, d, 2 * CONV_DIM), d ** -0.5),
        "odd_dw_w": nrm(ks[11], (N_ODD, CONV_WIDTH, CONV_DIM), CONV_WIDTH ** -0.5),
        "odd_dw_b": nrm(ks[12], (N_ODD, CONV_DIM), 0.02),
        "odd_ln_g": 1.0 + nrm(ks[13], (N_ODD, CONV_DIM), 0.05),
        "odd_ln_b": nrm(ks[14], (N_ODD, CONV_DIM), 0.02),
        "odd_w_pw2": nrm(ks[15], (N_ODD, CONV_DIM, d), CONV_DIM ** -0.5),
        "ffn_norm_g": 1.0 + nrm(ks[16], (DEPTH, d), 0.05),
        "ffn_w_up": nrm(ks[17], (DEPTH, d, 2 * D_FF), d ** -0.5),
        "ffn_conv_w": nrm(ks[18], (DEPTH, FFN_CONV_WIDTH, 2 * D_FF), FFN_CONV_WIDTH ** -0.5),
        "ffn_conv_b": nrm(ks[19], (DEPTH, 2 * D_FF), 0.02),
        "ffn_w_down": nrm(ks[20], (DEPTH, D_FF, d), D_FF ** -0.5),
    }


def reference(x, even_norm_g, even_w_in, even_b_f, even_q_norm_g, even_k_norm_g, even_w_pool,
              even_pool_scale, even_w_out, odd_norm_g, odd_w_pw1, odd_dw_w, odd_dw_b, odd_ln_g,
              odd_ln_b, odd_w_pw2, ffn_norm_g, ffn_w_up, ffn_conv_w, ffn_conv_b, ffn_w_down):
    for layer in range(DEPTH):
        i = layer // 2
        if layer % 2 == 0:
            h = rms_norm(x, even_norm_g[i])
            x = x + even_mixer(h, even_w_in[i], even_b_f[i], even_q_norm_g[i], even_k_norm_g[i],
                               even_w_pool[i], even_pool_scale[i], even_w_out[i])
        else:
            h = rms_norm(x, odd_norm_g[i])
            x = x + conformer_conv(h, odd_w_pw1[i], odd_dw_w[i], odd_dw_b[i], odd_ln_g[i],
                                   odd_ln_b[i], odd_w_pw2[i])
        h = rms_norm(x, ffn_norm_g[layer])
        x = x + conv_ffn(h, ffn_w_up[layer], ffn_conv_w[layer], ffn_conv_b[layer], ffn_w_down[layer])
    return x
```

```python
import functools
import math

import jax
import jax.numpy as jnp
from jax import lax
from jax.experimental import pallas as pl
from jax.experimental.pallas import tpu as pltpu

D_MODEL = 1024
N_HEADS = 8
HEAD_DIM = 64
ATTN_DIM = N_HEADS * HEAD_DIM
POOL_WINDOWS = (2, 4, 8, 16)
POOL_GROUP_DIM = 128
POOL_DIM = len(POOL_WINDOWS) * POOL_GROUP_DIM
CONV_WIDTH = 31
D_FF = 2816
FFN_CONV_WIDTH = 3
EPS = 1e-6

LANES = 128
SUBLANES = 8
MXU_WIDTH = 256
POOL_HALO = 16
CONV_HALO = 32
NEG_BIG = -1e30
VMEM_LIMIT = 56 * 1024 * 1024

F32 = jnp.float32
BF16 = jnp.bfloat16


def _rms_norm(x, g):
    ms = jnp.mean(x * x, axis=-1, keepdims=True)
    return x * lax.rsqrt(ms + EPS) * g


def _dot(a, b):
    return jnp.dot(a, b, preferred_element_type=F32)


def _dot_nt(a, b):
    return lax.dot_general(a, b, (((1,), (1,)), ((), ())), preferred_element_type=F32)


def _split3(c):
    hi = c.astype(BF16).astype(F32)
    r = c - hi
    mid = r.astype(BF16).astype(F32)
    lo = (r - mid).astype(BF16).astype(F32)
    return hi, mid, lo


def _const_spec(shape):
    zeros = (0,) * len(shape)
    return pl.BlockSpec(shape, lambda *_: zeros, pipeline_mode=pl.Buffered(1))


def _even_in_kernel(x_ref, g_ref, wqk_ref, wvt_ref, wu_ref, wf_ref, bf_ref, qg_ref, kg_ref,
                    tril_ref, q_ref, k_ref, vt_ref, u_ref, carry_ref, *, tm):
    @pl.when(pl.program_id(1) == 0)
    def _():
        carry_ref[...] = jnp.zeros_like(carry_ref)

    hb = _rms_norm(x_ref[0], g_ref[...]).astype(BF16)
    qk = _dot(hb, wqk_ref[...])
    vt_ref[0] = _dot_nt(wvt_ref[...], hb).astype(BF16)
    u_ref[0] = _dot(hb, wu_ref[...])

    fl = _dot(hb, wf_ref[...]) + bf_ref[...]
    logf = jnp.minimum(fl, 0.0) - jnp.log1p(jnp.exp(-jnp.abs(fl)))
    tril = tril_ref[...]
    c = carry_ref[0:1, :]
    for term in _split3(logf):
        c = c + _dot(tril, term.astype(BF16))
    carry_ref[0:1, :] = c[tm - 1:tm, :]

    lane = lax.broadcasted_iota(jnp.int32, (tm, LANES), 1)
    low = lane < HEAD_DIM
    inv_dh = 1.0 / HEAD_DIM
    scale = 1.0 / math.sqrt(HEAD_DIM)
    for pair in range(N_HEADS // 2):
        for which, (gain_ref, out_ref) in enumerate(((qg_ref, q_ref), (kg_ref, k_ref))):
            col = which * ATTN_DIM + pair * LANES
            slab = qk[:, col:col + LANES]
            sq = slab * slab
            ss_lo = jnp.sum(jnp.where(low, sq, 0.0), axis=-1, keepdims=True)
            ss_hi = jnp.sum(jnp.where(low, 0.0, sq), axis=-1, keepdims=True)
            inv = jnp.where(low, lax.rsqrt(ss_lo * inv_dh + EPS), lax.rsqrt(ss_hi * inv_dh + EPS))
            normed = slab * inv * gain_ref[...]
            if which == 0:
                normed = normed * scale
            for sub in range(2):
                head = 2 * pair + sub
                base = HEAD_DIM if sub == 0 else 0
                hi, mid, lo = _split3(jnp.broadcast_to(c[:, head:head + 1], (tm, LANES)))
                if which == 1:
                    hi, mid, lo = -hi, -mid, -lo
                c_at = base if which == 0 else base + 3
                one_at = base + 3 if which == 0 else base
                aug = jnp.where(lane == c_at, hi, jnp.where(lane == c_at + 1, mid,
                      jnp.where(lane == c_at + 2, lo,
                      jnp.where((lane >= one_at) & (lane < one_at + 3), 1.0, 0.0))))
                keep = low if sub == 0 else jnp.logical_not(low)
                out_ref[0, head] = jnp.where(keep, normed, aug).astype(BF16)


def _even_in(x, g, wqk, wvt, wu, wf, bf, qg, kg, tril, *, tm):
    b, s, d = x.shape
    kern = functools.partial(_even_in_kernel, tm=tm)
    return pl.pallas_call(
        kern,
        grid=(b, s // tm),
        in_specs=[
            pl.BlockSpec((1, tm, d), lambda bi, i: (bi, i, 0)),
            _const_spec(g.shape), _const_spec(wqk.shape), _const_spec(wvt.shape),
            _const_spec(wu.shape), _const_spec(wf.shape), _const_spec(bf.shape),
            _const_spec(qg.shape), _const_spec(kg.shape), _const_spec(tril.shape),
        ],
        out_specs=[
            pl.BlockSpec((1, N_HEADS, tm, LANES), lambda bi, i: (bi, 0, i, 0)),
            pl.BlockSpec((1, N_HEADS, tm, LANES), lambda bi, i: (bi, 0, i, 0)),
            pl.BlockSpec((1, ATTN_DIM, tm), lambda bi, i: (bi, 0, i)),
            pl.BlockSpec((1, tm, POOL_DIM), lambda bi, i: (bi, i, 0)),
        ],
        out_shape=[
            jax.ShapeDtypeStruct((b, N_HEADS, s, LANES), BF16),
            jax.ShapeDtypeStruct((b, N_HEADS, s, LANES), BF16),
            jax.ShapeDtypeStruct((b, ATTN_DIM, s), BF16),
            jax.ShapeDtypeStruct((b, s, POOL_DIM), F32),
        ],
        scratch_shapes=[pltpu.VMEM((SUBLANES, LANES), F32)],
        compiler_params=pltpu.CompilerParams(
            dimension_semantics=("arbitrary", "arbitrary"), vmem_limit_bytes=VMEM_LIMIT),
        name="even_in",
    )(x, g, wqk, wvt, wu, wf, bf, qg, kg, tril)


def _attn_kernel(q_ref, k_ref, vt_ref, o_ref, *, blk):
    i = pl.program_id(2)
    qs = [q_ref[0, hh] for hh in range(2)]

    def update(state, hh, st, start):
        m, l, acc = state
        m_new = jnp.maximum(m, jnp.max(st, axis=0, keepdims=True))
        alpha = jnp.exp(m - m_new)
        p = jnp.exp(st - m_new)
        l = alpha * l + jnp.sum(p, axis=0, keepdims=True)
        vb = vt_ref[0, hh * HEAD_DIM:(hh + 1) * HEAD_DIM, pl.ds(start, blk)]
        acc = alpha * acc + _dot(vb, p.astype(BF16))
        return m_new, l, acc

    def body(j, states):
        start = pl.multiple_of(j * blk, blk)
        out = []
        for hh in range(2):
            kb = k_ref[0, hh, pl.ds(start, blk), :]
            out.append(update(states[hh], hh, _dot_nt(kb, qs[hh]), start))
        return tuple(out)

    init = tuple((jnp.full((1, blk), NEG_BIG, F32), jnp.zeros((1, blk), F32),
                  jnp.zeros((HEAD_DIM, blk), F32)) for _ in range(2))
    states = lax.fori_loop(0, i, body, init)

    start = pl.multiple_of(i * blk, blk)
    causal = (lax.broadcasted_iota(jnp.int32, (blk, blk), 0)
              <= lax.broadcasted_iota(jnp.int32, (blk, blk), 1))
    outs = []
    for hh in range(2):
        kb = k_ref[0, hh, pl.ds(start, blk), :]
        st = jnp.where(causal, _dot_nt(kb, qs[hh]), NEG_BIG)
        m, l, acc = update(states[hh], hh, st, start)
        outs.append(acc / l)
    o_ref[0] = jnp.concatenate(outs, axis=0).T.astype(BF16)


def _fox_attention(q, k, vt, *, blk):
    b, nh, s, _ = q.shape
    kern = functools.partial(_attn_kernel, blk=blk)
    return pl.pallas_call(
        kern,
        grid=(b, nh // 2, s // blk),
        in_specs=[
            pl.BlockSpec((1, 2, blk, LANES), lambda bi, p, i: (bi, p, i, 0)),
            pl.BlockSpec((1, 2, s, LANES), lambda bi, p, i: (bi, p, 0, 0)),
            pl.BlockSpec((1, 2 * HEAD_DIM, s), lambda bi, p, i: (bi, p, 0)),
        ],
        out_specs=pl.BlockSpec((1, blk, 2 * HEAD_DIM), lambda bi, p, i: (bi, i, p)),
        out_shape=jax.ShapeDtypeStruct((b, s, ATTN_DIM), BF16),
        compiler_params=pltpu.CompilerParams(
            dimension_semantics=("arbitrary", "arbitrary", "arbitrary"),
            vmem_limit_bytes=VMEM_LIMIT),
        name="fox_attention",
    )(q, k, vt)


def _even_out_kernel(a_ref, u_ref, x_ref, wpool_ref, pscale_ref, wout_ref, o_ref, carry_ref, *, tm):
    i = pl.program_id(1)

    @pl.when(i == 0)
    def _():
        carry_ref[...] = jnp.zeros_like(carry_ref)

    u = u_ref[0]
    pos = (i * tm + 1 + lax.broadcasted_iota(jnp.int32, (tm, 1), 0)).astype(F32)
    parts = [a_ref[0]]
    for g, w in enumerate(POOL_WINDOWS):
        cols = slice(g * POOL_GROUP_DIM, (g + 1) * POOL_GROUP_DIM)
        ug = u[:, cols]
        run = jnp.concatenate([carry_ref[:, cols], ug], axis=0)
        sh = 1
        while sh < w:
            run = run + pltpu.roll(run, sh, axis=0)
            sh *= 2
        pooled = run[POOL_HALO:] / jnp.minimum(pos, float(w))
        mixed = (pooled - ug).astype(BF16)
        parts.append((_dot(mixed, wpool_ref[g]) * pscale_ref[:, cols]).astype(BF16))
    carry_ref[...] = u[tm - POOL_HALO:, :]
    cat = jnp.concatenate(parts, axis=1)
    o_ref[0] = x_ref[0] + _dot(cat, wout_ref[...])


def _even_out(a, u, x, wpool, pscale, wout, *, tm):
    b, s, d = x.shape
    kern = functools.partial(_even_out_kernel, tm=tm)
    return pl.pallas_call(
        kern,
        grid=(b, s // tm),
        in_specs=[
            pl.BlockSpec((1, tm, ATTN_DIM), lambda bi, i: (bi, i, 0)),
            pl.BlockSpec((1, tm, POOL_DIM), lambda bi, i: (bi, i, 0)),
            pl.BlockSpec((1, tm, d), lambda bi, i: (bi, i, 0)),
            _const_spec(wpool.shape), _const_spec(pscale.shape), _const_spec(wout.shape),
        ],
        out_specs=pl.BlockSpec((1, tm, d), lambda bi, i: (bi, i, 0)),
        out_shape=jax.ShapeDtypeStruct((b, s, d), F32),
        scratch_shapes=[pltpu.VMEM((POOL_HALO, POOL_DIM), F32)],
        compiler_params=pltpu.CompilerParams(
            dimension_semantics=("arbitrary", "arbitrary"), vmem_limit_bytes=VMEM_LIMIT),
        name="even_out",
    )(a, u, x, wpool, pscale, wout)


def _conformer_kernel(x_ref, g_ref, w1_ref, dww_ref, dwb_ref, lng_ref, lnb_ref, w2_ref, o_ref,
                      ext_ref, y_ref, *, tm):
    i = pl.program_id(1)

    @pl.when(i == 0)
    def _():
        ext_ref[0:CONV_HALO, :] = jnp.zeros((CONV_HALO, D_MODEL), F32)

    @pl.when(i > 0)
    def _():
        ext_ref[0:CONV_HALO, :] = ext_ref[tm:tm + CONV_HALO, :]

    x = x_ref[0]
    hb = _rms_norm(x, g_ref[...]).astype(BF16)
    ag = _dot(hb, w1_ref[...])
    ext_ref[CONV_HALO:, :] = ag[:, :D_MODEL] * jax.nn.sigmoid(ag[:, D_MODEL:])

    def conv_chunk(c, carry):
        cols = pl.ds(pl.multiple_of(c * LANES, LANES), LANES)
        e = ext_ref[:, cols]
        acc = jnp.broadcast_to(dwb_ref[:, cols], (tm, LANES))
        for r in range(SUBLANES):
            er = e if r == 0 else pltpu.roll(e, r, axis=0)
            for a in range(CONV_HALO // SUBLANES):
                shift = SUBLANES * a + r
                if shift >= CONV_WIDTH:
                    continue
                tap = CONV_WIDTH - 1 - shift
                lo = CONV_HALO - SUBLANES * a
                acc = acc + dww_ref[pl.ds(tap, 1), cols] * er[lo:lo + tm]
        y_ref[:, cols] = acc
        return carry

    lax.fori_loop(0, D_MODEL // LANES, conv_chunk, 0)

    y = y_ref[...]
    mu = jnp.mean(y, axis=-1, keepdims=True)
    yc = y - mu
    var = jnp.mean(yc * yc, axis=-1, keepdims=True)
    z = yc * lax.rsqrt(var + EPS) * lng_ref[...] + lnb_ref[...]
    z = z * jax.nn.sigmoid(z)
    o_ref[0] = x + _dot(z.astype(BF16), w2_ref[...])


def _conformer(x, g, w1, dww, dwb, lng, lnb, w2, *, tm):
    b, s, d = x.shape
    kern = functools.partial(_conformer_kernel, tm=tm)
    return pl.pallas_call(
        kern,
        grid=(b, s // tm),
        in_specs=[
            pl.BlockSpec((1, tm, d), lambda bi, i: (bi, i, 0)),
            _const_spec(g.shape), _const_spec(w1.shape), _const_spec(dww.shape),
            _const_spec(dwb.shape), _const_spec(lng.shape), _const_spec(lnb.shape),
            _const_spec(w2.shape),
        ],
        out_specs=pl.BlockSpec((1, tm, d), lambda bi, i: (bi, i, 0)),
        out_shape=jax.ShapeDtypeStruct((b, s, d), F32),
        scratch_shapes=[pltpu.VMEM((CONV_HALO + tm, D_MODEL), F32), pltpu.VMEM((tm, D_MODEL), F32)],
        compiler_params=pltpu.CompilerParams(
            dimension_semantics=("arbitrary", "arbitrary"), vmem_limit_bytes=VMEM_LIMIT),
        name="conformer",
    )(x, g, w1, dww, dwb, lng, lnb, w2)


def _conv_ffn_kernel(x_ref, g_ref, wup_ref, cw_ref, cb_ref, wdown_ref, o_ref, act_ref, carry_ref,
                     *, tm, chunk):
    @pl.when(pl.program_id(1) == 0)
    def _():
        carry_ref[...] = jnp.zeros_like(carry_ref)

    x = x_ref[0]
    hb = _rms_norm(x, g_ref[...]).astype(BF16)

    def conv_cols(col):
        cols = slice(col, col + chunk)
        up = _dot(hb, wup_ref[:, cols])
        ext = jnp.concatenate([carry_ref[:, cols], up], axis=0)
        carry_ref[:, cols] = up[tm - SUBLANES:, :]
        y = cb_ref[:, cols] + cw_ref[2:3, cols] * up
        y = y + cw_ref[1:2, cols] * pltpu.roll(ext, 1, axis=0)[SUBLANES:]
        y = y + cw_ref[0:1, cols] * pltpu.roll(ext, 2, axis=0)[SUBLANES:]
        return y

    for j in range(D_FF // chunk):
        gate = conv_cols(j * chunk)
        val = conv_cols(D_FF + j * chunk)
        act_ref[:, j * chunk:(j + 1) * chunk] = (gate * jax.nn.sigmoid(gate) * val).astype(BF16)

    o_ref[0] = x + _dot(act_ref[...], wdown_ref[...])


def _conv_ffn(x, g, wup, cw, cb, wdown, *, tm, chunk):
    b, s, d = x.shape
    kern = functools.partial(_conv_ffn_kernel, tm=tm, chunk=chunk)
    return pl.pallas_call(
        kern,
        grid=(b, s // tm),
        in_specs=[
            pl.BlockSpec((1, tm, d), lambda bi, i: (bi, i, 0)),
            _const_spec(g.shape), _const_spec(wup.shape), _const_spec(cw.shape),
            _const_spec(cb.shape), _const_spec(wdown.shape),
        ],
        out_specs=pl.BlockSpec((1, tm, d), lambda bi, i: (bi, i, 0)),
        out_shape=jax.ShapeDtypeStruct((b, s, d), F32),
        scratch_shapes=[pltpu.VMEM((tm, D_FF), BF16), pltpu.VMEM((SUBLANES, 2 * D_FF), F32)],
        compiler_params=pltpu.CompilerParams(
            dimension_semantics=("arbitrary", "arbitrary"), vmem_limit_bytes=VMEM_LIMIT),
        name="conv_ffn",
    )(x, g, wup, cw, cb, wdown)


def _pad_rows(a, rows):
    return jnp.pad(a, ((0, rows - a.shape[0]), (0, 0)))


def kernel(x, even_norm_g, even_w_in, even_b_f, even_q_norm_g, even_k_norm_g, even_w_pool,
           even_pool_scale, even_w_out, odd_norm_g, odd_w_pw1, odd_dw_w, odd_dw_b, odd_ln_g,
           odd_ln_b, odd_w_pw2, ffn_norm_g, ffn_w_up, ffn_conv_w, ffn_conv_b, ffn_w_down):
    depth = ffn_norm_g.shape[0]
    seq = x.shape[1]
    tm = min(512, seq)
    blk = min(512, seq)
    tril = jnp.tril(jnp.ones((tm, tm), BF16))
    row = lambda v: v.reshape(1, -1).astype(F32)

    for layer in range(depth):
        i = layer // 2
        if layer % 2 == 0:
            w_in = even_w_in[i]
            o = 2 * ATTN_DIM
            wqk = w_in[:, :o].astype(BF16)
            wvt = w_in[:, o:o + ATTN_DIM].T.astype(BF16)
            o += ATTN_DIM
            wf = jnp.pad(w_in[:, o:o + N_HEADS], ((0, 0), (0, LANES - N_HEADS))).astype(BF16)
            bf = jnp.pad(even_b_f[i], (0, LANES - N_HEADS)).reshape(1, LANES).astype(F32)
            o += N_HEADS
            wu = w_in[:, o:o + POOL_DIM].astype(BF16)
            qg = row(jnp.tile(even_q_norm_g[i], 2))
            kg = row(jnp.tile(even_k_norm_g[i], 2))
            q, k, vt, u = _even_in(x, row(even_norm_g[i]), wqk, wvt, wu, wf, bf, qg, kg, tril, tm=tm)
            a = _fox_attention(q, k, vt, blk=blk)
            x = _even_out(a, u, x, even_w_pool[i].astype(BF16), row(even_pool_scale[i]),
                          even_w_out[i].astype(BF16), tm=tm)
        else:
            x = _conformer(x, row(odd_norm_g[i]), odd_w_pw1[i].astype(BF16),
                           _pad_rows(odd_dw_w[i].astype(F32), CONV_HALO), row(odd_dw_b[i]),
                           row(odd_ln_g[i]), row(odd_ln_b[i]), odd_w_pw2[i].astype(BF16), tm=tm)
        x = _conv_ffn(x, row(ffn_norm_g[layer]), ffn_w_up[layer].astype(BF16),
                      _pad_rows(ffn_conv_w[layer].astype(F32), SUBLANES), row(ffn_conv_b[layer]),
                      ffn_w_down[layer].astype(BF16), tm=tm, chunk=MXU_WIDTH)
    return x
```

```python
import functools
import math

import jax
import jax.numpy as jnp
from jax import lax
from jax.experimental import pallas as pl
from jax.experimental.pallas import tpu as pltpu

D_MODEL = 1024
N_HEADS = 8
HEAD_DIM = 64
ATTN_DIM = N_HEADS * HEAD_DIM
POOL_WINDOWS = (2, 4, 8, 16)
POOL_GROUP_DIM = 128
POOL_DIM = len(POOL_WINDOWS) * POOL_GROUP_DIM
CONV_WIDTH = 31
D_FF = 2816
FFN_CONV_WIDTH = 3
EPS = 1e-6

LANES = 128
SUBLANES = 8
MXU_WIDTH = 256
POOL_HALO = 16
CONV_HALO = 32
NEG_BIG = -1e30
LOG2E = math.log2(math.e)
BF16_SUBLANES = 16
VT_ROWS = HEAD_DIM + BF16_SUBLANES
VMEM_LIMIT = 56 * 1024 * 1024

F32 = jnp.float32
BF16 = jnp.bfloat16


def _rms_norm(x, g):
    ms = jnp.mean(x * x, axis=-1, keepdims=True)
    return x * lax.rsqrt(ms + EPS) * g


def _dot(a, b):
    return jnp.dot(a, b, preferred_element_type=F32)


def _dot_nt(a, b):
    return lax.dot_general(a, b, (((1,), (1,)), ((), ())), preferred_element_type=F32)


def _split3(c):
    hi = c.astype(BF16).astype(F32)
    r = c - hi
    mid = r.astype(BF16).astype(F32)
    lo = (r - mid).astype(BF16).astype(F32)
    return hi, mid, lo


def _const_spec(shape):
    zeros = (0,) * len(shape)
    return pl.BlockSpec(shape, lambda *_: zeros, pipeline_mode=pl.Buffered(1))


def _even_in_kernel(x_ref, g_ref, wqk_ref, wvt_ref, wu_ref, wf_ref, bf_ref, qg_ref, kg_ref,
                    tril_ref, q_ref, k_ref, vt_ref, u_ref, carry_ref, *, tm):
    @pl.when(pl.program_id(1) == 0)
    def _():
        carry_ref[...] = jnp.zeros_like(carry_ref)

    hb = _rms_norm(x_ref[0], g_ref[...]).astype(BF16)
    qk = _dot(hb, wqk_ref[...])
    vt = _dot_nt(wvt_ref[...], hb).astype(BF16)
    for head in range(N_HEADS):
        vt_ref[0, head * VT_ROWS:head * VT_ROWS + HEAD_DIM] = vt[head * HEAD_DIM:(head + 1) * HEAD_DIM]
        vt_ref[0, head * VT_ROWS + HEAD_DIM:(head + 1) * VT_ROWS] = jnp.ones(
            (VT_ROWS - HEAD_DIM, tm), BF16)
    u_ref[0] = _dot(hb, wu_ref[...])

    fl = _dot(hb, wf_ref[...]) + bf_ref[...]
    logf = jnp.minimum(fl, 0.0) - jnp.log1p(jnp.exp(-jnp.abs(fl)))
    tril = tril_ref[...]
    c = carry_ref[0:1, :]
    for term in _split3(logf):
        c = c + _dot(tril, term.astype(BF16))
    carry_ref[0:1, :] = c[tm - 1:tm, :]
    c = c * LOG2E

    lane = lax.broadcasted_iota(jnp.int32, (tm, LANES), 1)
    low = lane < HEAD_DIM
    inv_dh = 1.0 / HEAD_DIM
    scale = LOG2E / math.sqrt(HEAD_DIM)
    for pair in range(N_HEADS // 2):
        for which, (gain_ref, out_ref) in enumerate(((qg_ref, q_ref), (kg_ref, k_ref))):
            col = which * ATTN_DIM + pair * LANES
            slab = qk[:, col:col + LANES]
            sq = slab * slab
            ss_lo = jnp.sum(jnp.where(low, sq, 0.0), axis=-1, keepdims=True)
            ss_hi = jnp.sum(jnp.where(low, 0.0, sq), axis=-1, keepdims=True)
            inv = jnp.where(low, lax.rsqrt(ss_lo * inv_dh + EPS), lax.rsqrt(ss_hi * inv_dh + EPS))
            normed = slab * inv * gain_ref[...]
            if which == 0:
                normed = normed * scale
            for sub in range(2):
                head = 2 * pair + sub
                base = HEAD_DIM if sub == 0 else 0
                hi, mid, lo = _split3(jnp.broadcast_to(c[:, head:head + 1], (tm, LANES)))
                if which == 1:
                    hi, mid, lo = -hi, -mid, -lo
                c_at = base if which == 0 else base + 3
                one_at = base + 3 if which == 0 else base
                aug = jnp.where(lane == c_at, hi, jnp.where(lane == c_at + 1, mid,
                      jnp.where(lane == c_at + 2, lo,
                      jnp.where((lane >= one_at) & (lane < one_at + 3), 1.0, 0.0))))
                keep = low if sub == 0 else jnp.logical_not(low)
                out_ref[0, head] = jnp.where(keep, normed, aug).astype(BF16)


def _even_in(x, g, wqk, wvt, wu, wf, bf, qg, kg, tril, *, tm):
    b, s, d = x.shape
    kern = functools.partial(_even_in_kernel, tm=tm)
    return pl.pallas_call(
        kern,
        grid=(b, s // tm),
        in_specs=[
            pl.BlockSpec((1, tm, d), lambda bi, i: (bi, i, 0)),
            _const_spec(g.shape), _const_spec(wqk.shape), _const_spec(wvt.shape),
            _const_spec(wu.shape), _const_spec(wf.shape), _const_spec(bf.shape),
            _const_spec(qg.shape), _const_spec(kg.shape), _const_spec(tril.shape),
        ],
        out_specs=[
            pl.BlockSpec((1, N_HEADS, tm, LANES), lambda bi, i: (bi, 0, i, 0)),
            pl.BlockSpec((1, N_HEADS, tm, LANES), lambda bi, i: (bi, 0, i, 0)),
            pl.BlockSpec((1, N_HEADS * VT_ROWS, tm), lambda bi, i: (bi, 0, i)),
            pl.BlockSpec((1, tm, POOL_DIM), lambda bi, i: (bi, i, 0)),
        ],
        out_shape=[
            jax.ShapeDtypeStruct((b, N_HEADS, s, LANES), BF16),
            jax.ShapeDtypeStruct((b, N_HEADS, s, LANES), BF16),
            jax.ShapeDtypeStruct((b, N_HEADS * VT_ROWS, s), BF16),
            jax.ShapeDtypeStruct((b, s, POOL_DIM), F32),
        ],
        scratch_shapes=[pltpu.VMEM((SUBLANES, LANES), F32)],
        compiler_params=pltpu.CompilerParams(
            dimension_semantics=("arbitrary", "arbitrary"), vmem_limit_bytes=VMEM_LIMIT),
        name="even_in",
    )(x, g, wqk, wvt, wu, wf, bf, qg, kg, tril)


def _attn_kernel(q_ref, k_ref, vt_ref, o_ref, s0_ref, s1_ref, *, bq):
    bk = bq // 2
    i = pl.program_id(2)
    row = lax.broadcasted_iota(jnp.int32, (bk, bq), 0)
    col = lax.broadcasted_iota(jnp.int32, (bk, bq), 1)
    outs = []
    for hh in range(2):
        q = q_ref[0, hh]

        def scores(blk_idx, q=q, hh=hh):
            start = pl.multiple_of(blk_idx * bk, bk)
            return _dot_nt(k_ref[0, hh, pl.ds(start, bk), :], q)

        def accumulate(m, acc, s, mb, blk_idx, hh=hh):
            m_new = jnp.maximum(m, mb)
            p = jnp.exp2(s - m_new).astype(BF16)
            start = pl.multiple_of(blk_idx * bk, bk)
            vb = vt_ref[0, hh * VT_ROWS:(hh + 1) * VT_ROWS, pl.ds(start, bk)]
            return m_new, jnp.exp2(m - m_new) * acc + _dot(vb, p)

        def body(jj, carry):
            m, acc, mb0 = carry
            s1 = scores(2 * jj + 1)
            s1_ref[...] = s1
            mb1 = jnp.max(s1, axis=0, keepdims=True)
            m, acc = accumulate(m, acc, s0_ref[...], mb0, 2 * jj)
            s0 = scores(2 * jj + 2)
            s0_ref[...] = s0
            mb0 = jnp.max(s0, axis=0, keepdims=True)
            m, acc = accumulate(m, acc, s1_ref[...], mb1, 2 * jj + 1)
            return m, acc, mb0

        s0 = scores(0)
        s0_ref[...] = s0
        init = (jnp.full((1, bq), NEG_BIG, F32), jnp.zeros((VT_ROWS, bq), F32),
                jnp.max(s0, axis=0, keepdims=True))
        m, acc, _ = lax.fori_loop(0, i, body, init)

        s = jnp.where(row <= col, s0_ref[...], NEG_BIG)
        m, acc = accumulate(m, acc, s, jnp.max(s, axis=0, keepdims=True), 2 * i)
        s = jnp.where(row + bk <= col, scores(2 * i + 1), NEG_BIG)
        m, acc = accumulate(m, acc, s, jnp.max(s, axis=0, keepdims=True), 2 * i + 1)
        outs.append(acc[:HEAD_DIM] / acc[HEAD_DIM:HEAD_DIM + 1])
    o_ref[0] = jnp.concatenate(outs, axis=0).T.astype(BF16)


def _fox_attention(q, k, vt, *, bq):
    b, nh, s, _ = q.shape
    kern = functools.partial(_attn_kernel, bq=bq)
    return pl.pallas_call(
        kern,
        grid=(b, nh // 2, s // bq),
        in_specs=[
            pl.BlockSpec((1, 2, bq, LANES), lambda bi, p, i: (bi, p, i, 0)),
            pl.BlockSpec((1, 2, s, LANES), lambda bi, p, i: (bi, p, 0, 0)),
            pl.BlockSpec((1, 2 * VT_ROWS, s), lambda bi, p, i: (bi, p, 0)),
        ],
        out_specs=pl.BlockSpec((1, bq, 2 * HEAD_DIM), lambda bi, p, i: (bi, i, p)),
        out_shape=jax.ShapeDtypeStruct((b, s, ATTN_DIM), BF16),
        scratch_shapes=[pltpu.VMEM((bq // 2, bq), F32), pltpu.VMEM((bq // 2, bq), F32)],
        compiler_params=pltpu.CompilerParams(
            dimension_semantics=("arbitrary", "arbitrary", "arbitrary"),
            vmem_limit_bytes=VMEM_LIMIT),
        name="fox_attention",
    )(q, k, vt)


def _even_out_kernel(a_ref, u_ref, x_ref, wpool_ref, pscale_ref, wout_ref, o_ref, carry_ref, *, tm):
    i = pl.program_id(1)

    @pl.when(i == 0)
    def _():
        carry_ref[...] = jnp.zeros_like(carry_ref)

    u = u_ref[0]
    pos = (i * tm + 1 + lax.broadcasted_iota(jnp.int32, (tm, 1), 0)).astype(F32)
    parts = [a_ref[0]]
    for g, w in enumerate(POOL_WINDOWS):
        cols = slice(g * POOL_GROUP_DIM, (g + 1) * POOL_GROUP_DIM)
        ug = u[:, cols]
        run = jnp.concatenate([carry_ref[:, cols], ug], axis=0)
        sh = 1
        while sh < w:
            run = run + pltpu.roll(run, sh, axis=0)
            sh *= 2
        pooled = run[POOL_HALO:] / jnp.minimum(pos, float(w))
        mixed = (pooled - ug).astype(BF16)
        parts.append((_dot(mixed, wpool_ref[g]) * pscale_ref[:, cols]).astype(BF16))
    carry_ref[...] = u[tm - POOL_HALO:, :]
    cat = jnp.concatenate(parts, axis=1)
    o_ref[0] = x_ref[0] + _dot(cat, wout_ref[...])


def _even_out(a, u, x, wpool, pscale, wout, *, tm):
    b, s, d = x.shape
    kern = functools.partial(_even_out_kernel, tm=tm)
    return pl.pallas_call(
        kern,
        grid=(b, s // tm),
        in_specs=[
            pl.BlockSpec((1, tm, ATTN_DIM), lambda bi, i: (bi, i, 0)),
            pl.BlockSpec((1, tm, POOL_DIM), lambda bi, i: (bi, i, 0)),
            pl.BlockSpec((1, tm, d), lambda bi, i: (bi, i, 0)),
            _const_spec(wpool.shape), _const_spec(pscale.shape), _const_spec(wout.shape),
        ],
        out_specs=pl.BlockSpec((1, tm, d), lambda bi, i: (bi, i, 0)),
        out_shape=jax.ShapeDtypeStruct((b, s, d), F32),
        scratch_shapes=[pltpu.VMEM((POOL_HALO, POOL_DIM), F32)],
        compiler_params=pltpu.CompilerParams(
            dimension_semantics=("arbitrary", "arbitrary"), vmem_limit_bytes=VMEM_LIMIT),
        name="even_out",
    )(a, u, x, wpool, pscale, wout)


def _conformer_kernel(x_ref, g_ref, w1_ref, dww_ref, dwb_ref, lng_ref, lnb_ref, w2_ref, o_ref,
                      ext_ref, y_ref, *, tm):
    i = pl.program_id(1)

    @pl.when(i == 0)
    def _():
        ext_ref[0:CONV_HALO, :] = jnp.zeros((CONV_HALO, D_MODEL), F32)

    @pl.when(i > 0)
    def _():
        ext_ref[0:CONV_HALO, :] = ext_ref[tm:tm + CONV_HALO, :]

    x = x_ref[0]
    hb = _rms_norm(x, g_ref[...]).astype(BF16)
    ag = _dot(hb, w1_ref[...])
    ext_ref[CONV_HALO:, :] = ag[:, :D_MODEL] * jax.nn.sigmoid(ag[:, D_MODEL:])

    def conv_chunk(c, carry):
        cols = pl.ds(pl.multiple_of(c * LANES, LANES), LANES)
        e = ext_ref[:, cols]
        acc = jnp.broadcast_to(dwb_ref[:, cols], (tm, LANES))
        for r in range(SUBLANES):
            er = e if r == 0 else pltpu.roll(e, r, axis=0)
            for a in range(CONV_HALO // SUBLANES):
                shift = SUBLANES * a + r
                if shift >= CONV_WIDTH:
                    continue
                tap = CONV_WIDTH - 1 - shift
                lo = CONV_HALO - SUBLANES * a
                acc = acc + dww_ref[pl.ds(tap, 1), cols] * er[lo:lo + tm]
        y_ref[:, cols] = acc
        return carry

    lax.fori_loop(0, D_MODEL // LANES, conv_chunk, 0)

    y = y_ref[...]
    mu = jnp.mean(y, axis=-1, keepdims=True)
    yc = y - mu
    var = jnp.mean(yc * yc, axis=-1, keepdims=True)
    z = yc * lax.rsqrt(var + EPS) * lng_ref[...] + lnb_ref[...]
    z = z * jax.nn.sigmoid(z)
    o_ref[0] = x + _dot(z.astype(BF16), w2_ref[...])


def _conformer(x, g, w1, dww, dwb, lng, lnb, w2, *, tm):
    b, s, d = x.shape
    kern = functools.partial(_conformer_kernel, tm=tm)
    return pl.pallas_call(
        kern,
        grid=(b, s // tm),
        in_specs=[
            pl.BlockSpec((1, tm, d), lambda bi, i: (bi, i, 0)),
            _const_spec(g.shape), _const_spec(w1.shape), _const_spec(dww.shape),
            _const_spec(dwb.shape), _const_spec(lng.shape), _const_spec(lnb.shape),
            _const_spec(w2.shape),
        ],
        out_specs=pl.BlockSpec((1, tm, d), lambda bi, i: (bi, i, 0)),
        out_shape=jax.ShapeDtypeStruct((b, s, d), F32),
        scratch_shapes=[pltpu.VMEM((CONV_HALO + tm, D_MODEL), F32), pltpu.VMEM((tm, D_MODEL), F32)],
        compiler_params=pltpu.CompilerParams(
            dimension_semantics=("arbitrary", "arbitrary"), vmem_limit_bytes=VMEM_LIMIT),
        name="conformer",
    )(x, g, w1, dww, dwb, lng, lnb, w2)


def _conv_ffn_kernel(x_ref, g_ref, wup_ref, cw_ref, cb_ref, wdown_ref, o_ref, act_ref, carry_ref,
                     *, tm, chunk):
    @pl.when(pl.program_id(1) == 0)
    def _():
        carry_ref[...] = jnp.zeros_like(carry_ref)

    x = x_ref[0]
    hb = _rms_norm(x, g_ref[...]).astype(BF16)

    def conv_cols(col):
        cols = slice(col, col + chunk)
        up = _dot(hb, wup_ref[:, cols])
        ext = jnp.concatenate([carry_ref[:, cols], up], axis=0)
        carry_ref[:, cols] = up[tm - SUBLANES:, :]
        y = cb_ref[:, cols] + cw_ref[2:3, cols] * up
        y = y + cw_ref[1:2, cols] * pltpu.roll(ext, 1, axis=0)[SUBLANES:]
        y = y + cw_ref[0:1, cols] * pltpu.roll(ext, 2, axis=0)[SUBLANES:]
        return y

    for j in range(D_FF // chunk):
        gate = conv_cols(j * chunk)
        val = conv_cols(D_FF + j * chunk)
        act_ref[:, j * chunk:(j + 1) * chunk] = (gate * jax.nn.sigmoid(gate) * val).astype(BF16)

    o_ref[0] = x + _dot(act_ref[...], wdown_ref[...])


def _conv_ffn(x, g, wup, cw, cb, wdown, *, tm, chunk):
    b, s, d = x.shape
    kern = functools.partial(_conv_ffn_kernel, tm=tm, chunk=chunk)
    return pl.pallas_call(
        kern,
        grid=(b, s // tm),
        in_specs=[
            pl.BlockSpec((1, tm, d), lambda bi, i: (bi, i, 0)),
            _const_spec(g.shape), _const_spec(wup.shape), _const_spec(cw.shape),
            _const_spec(cb.shape), _const_spec(wdown.shape),
        ],
        out_specs=pl.BlockSpec((1, tm, d), lambda bi, i: (bi, i, 0)),
        out_shape=jax.ShapeDtypeStruct((b, s, d), F32),
        scratch_shapes=[pltpu.VMEM((tm, D_FF), BF16), pltpu.VMEM((SUBLANES, 2 * D_FF), F32)],
        compiler_params=pltpu.CompilerParams(
            dimension_semantics=("arbitrary", "arbitrary"), vmem_limit_bytes=VMEM_LIMIT),
        name="conv_ffn",
    )(x, g, wup, cw, cb, wdown)


def _pad_rows(a, rows):
    return jnp.pad(a, ((0, rows - a.shape[0]), (0, 0)))


def kernel(x, even_norm_g, even_w_in, even_b_f, even_q_norm_g, even_k_norm_g, even_w_pool,
           even_pool_scale, even_w_out, odd_norm_g, odd_w_pw1, odd_dw_w, odd_dw_b, odd_ln_g,
           odd_ln_b, odd_w_pw2, ffn_norm_g, ffn_w_up, ffn_conv_w, ffn_conv_b, ffn_w_down):
    depth = ffn_norm_g.shape[0]
    seq = x.shape[1]
    tm = min(512, seq)
    blk = min(1024, seq)
    tril = jnp.tril(jnp.ones((tm, tm), BF16))
    row = lambda v: v.reshape(1, -1).astype(F32)

    for layer in range(depth):
        i = layer // 2
        if layer % 2 == 0:
            w_in = even_w_in[i]
            o = 2 * ATTN_DIM
            wqk = w_in[:, :o].astype(BF16)
            wvt = w_in[:, o:o + ATTN_DIM].T.astype(BF16)
            o += ATTN_DIM
            wf = jnp.pad(w_in[:, o:o + N_HEADS], ((0, 0), (0, LANES - N_HEADS))).astype(BF16)
            bf = jnp.pad(even_b_f[i], (0, LANES - N_HEADS)).reshape(1, LANES).astype(F32)
            o += N_HEADS
            wu = w_in[:, o:o + POOL_DIM].astype(BF16)
            qg = row(jnp.tile(even_q_norm_g[i], 2))
            kg = row(jnp.tile(even_k_norm_g[i], 2))
            q, k, vt, u = _even_in(x, row(even_norm_g[i]), wqk, wvt, wu, wf, bf, qg, kg, tril, tm=tm)
            a = _fox_attention(q, k, vt, bq=blk)
            x = _even_out(a, u, x, even_w_pool[i].astype(BF16), row(even_pool_scale[i]),
                          even_w_out[i].astype(BF16), tm=tm)
        else:
            x = _conformer(x, row(odd_norm_g[i]), odd_w_pw1[i].astype(BF16),
                           _pad_rows(odd_dw_w[i].astype(F32), CONV_HALO), row(odd_dw_b[i]),
                           row(odd_ln_g[i]), row(odd_ln_b[i]), odd_w_pw2[i].astype(BF16), tm=tm)
        x = _conv_ffn(x, row(ffn_norm_g[layer]), ffn_w_up[layer].astype(BF16),
                      _pad_rows(ffn_conv_w[layer].astype(F32), SUBLANES), row(ffn_conv_b[layer]),
                      ffn_w_down[layer].astype(BF16), tm=tm, chunk=MXU_WIDTH)
    return x
```

```python
import functools
import math

import jax
import jax.numpy as jnp
import numpy as np
from jax import lax
from jax.experimental import pallas as pl
from jax.experimental.pallas import tpu as pltpu

D_MODEL = 1024
N_HEADS = 8
HEAD_DIM = 64
ATTN_DIM = N_HEADS * HEAD_DIM
POOL_WINDOWS = (2, 4, 8, 16)
POOL_GROUP_DIM = 128
POOL_DIM = len(POOL_WINDOWS) * POOL_GROUP_DIM
CONV_WIDTH = 31
D_FF = 2816
FFN_CONV_WIDTH = 3
EPS = 1e-6

LANES = 128
SUBLANES = 8
MXU_WIDTH = 256
POOL_HALO = 16
CONV_HALO = 32
NEG_BIG = -1e30
LOG2E = math.log2(math.e)
BF16_SUBLANES = 16
VT_ROWS = HEAD_DIM + BF16_SUBLANES
KEY_SPLIT = 2
VMEM_LIMIT = 56 * 1024 * 1024

F32 = jnp.float32
BF16 = jnp.bfloat16


def _rms_norm(x, g):
    ms = jnp.mean(x * x, axis=-1, keepdims=True)
    return x * lax.rsqrt(ms + EPS) * g


def _dot(a, b):
    return jnp.dot(a, b, preferred_element_type=F32)


def _dot_nt(a, b):
    return lax.dot_general(a, b, (((1,), (1,)), ((), ())), preferred_element_type=F32)


def _split3(c):
    hi = c.astype(BF16).astype(F32)
    r = c - hi
    mid = r.astype(BF16).astype(F32)
    lo = (r - mid).astype(BF16).astype(F32)
    return hi, mid, lo


def _const_spec(shape):
    zeros = (0,) * len(shape)
    return pl.BlockSpec(shape, lambda *_: zeros, pipeline_mode=pl.Buffered(1))


def _even_in_kernel(x_ref, g_ref, wqk_ref, wvt_ref, wu_ref, wf_ref, bf_ref, qg_ref, kg_ref,
                    tril_ref, place_ref, q_ref, k_ref, vt_ref, u_ref, carry_ref, *, tm):
    @pl.when(pl.program_id(1) == 0)
    def _():
        carry_ref[...] = jnp.zeros_like(carry_ref)

    hb = _rms_norm(x_ref[0], g_ref[...]).astype(BF16)
    qk = _dot(hb, wqk_ref[...])
    vt = _dot_nt(wvt_ref[...], hb).astype(BF16)
    for head in range(N_HEADS):
        vt_ref[0, head * VT_ROWS:head * VT_ROWS + HEAD_DIM] = vt[head * HEAD_DIM:(head + 1) * HEAD_DIM]
        vt_ref[0, head * VT_ROWS + HEAD_DIM:(head + 1) * VT_ROWS] = jnp.ones(
            (VT_ROWS - HEAD_DIM, tm), BF16)
    u_ref[0] = _dot(hb, wu_ref[...])

    fl = _dot(hb, wf_ref[...]) + bf_ref[...]
    logf = jnp.minimum(fl, 0.0) - jnp.log1p(jnp.exp(-jnp.abs(fl)))
    tril = tril_ref[...]
    c = carry_ref[0:1, :]
    for term in _split3(logf):
        c = c + _dot(tril, term.astype(BF16))
    carry_ref[0:1, :] = c[tm - 1:tm, :]
    c = c * LOG2E

    lane = lax.broadcasted_iota(jnp.int32, (tm, LANES), 1)
    hi, mid, lo = _split3(c)
    c3 = jnp.where(lane < N_HEADS, hi, jnp.where(lane < 2 * N_HEADS, mid,
         jnp.where(lane < 3 * N_HEADS, lo, jnp.where(lane == 3 * N_HEADS, 1.0, 0.0))))
    aug_qk = _dot(c3.astype(BF16), place_ref[...])

    low = lane < HEAD_DIM
    inv_dh = 1.0 / HEAD_DIM
    scale = LOG2E / math.sqrt(HEAD_DIM)
    for pair in range(N_HEADS // 2):
        for which, (gain_ref, out_ref) in enumerate(((qg_ref, q_ref), (kg_ref, k_ref))):
            col = which * ATTN_DIM + pair * LANES
            slab = qk[:, col:col + LANES]
            sq = slab * slab
            ss_lo = jnp.sum(jnp.where(low, sq, 0.0), axis=-1, keepdims=True)
            ss_hi = jnp.sum(jnp.where(low, 0.0, sq), axis=-1, keepdims=True)
            inv = jnp.where(low, lax.rsqrt(ss_lo * inv_dh + EPS), lax.rsqrt(ss_hi * inv_dh + EPS))
            normed = slab * inv * gain_ref[...]
            if which == 0:
                normed = normed * scale
            aug = aug_qk[:, col:col + LANES]
            out_ref[0, 2 * pair] = jnp.where(low, normed, aug).astype(BF16)
            out_ref[0, 2 * pair + 1] = jnp.where(low, aug, normed).astype(BF16)


def _gate_placement():
    place = np.zeros((LANES, 2 * ATTN_DIM), np.float32)
    for head in range(N_HEADS):
        base = (head // 2) * LANES + (HEAD_DIM if head % 2 == 0 else 0)
        for term in range(3):
            place[term * N_HEADS + head, base + term] = 1.0
            place[3 * N_HEADS, base + 3 + term] = 1.0
            place[3 * N_HEADS, ATTN_DIM + base + term] = 1.0
            place[term * N_HEADS + head, ATTN_DIM + base + 3 + term] = -1.0
    return jnp.asarray(place, BF16)


def _even_in(x, g, wqk, wvt, wu, wf, bf, qg, kg, tril, place, *, tm):
    b, s, d = x.shape
    kern = functools.partial(_even_in_kernel, tm=tm)
    return pl.pallas_call(
        kern,
        grid=(b, s // tm),
        in_specs=[
            pl.BlockSpec((1, tm, d), lambda bi, i: (bi, i, 0)),
            _const_spec(g.shape), _const_spec(wqk.shape), _const_spec(wvt.shape),
            _const_spec(wu.shape), _const_spec(wf.shape), _const_spec(bf.shape),
            _const_spec(qg.shape), _const_spec(kg.shape), _const_spec(tril.shape),
            _const_spec(place.shape),
        ],
        out_specs=[
            pl.BlockSpec((1, N_HEADS, tm, LANES), lambda bi, i: (bi, 0, i, 0)),
            pl.BlockSpec((1, N_HEADS, tm, LANES), lambda bi, i: (bi, 0, i, 0)),
            pl.BlockSpec((1, N_HEADS * VT_ROWS, tm), lambda bi, i: (bi, 0, i)),
            pl.BlockSpec((1, tm, POOL_DIM), lambda bi, i: (bi, i, 0)),
        ],
        out_shape=[
            jax.ShapeDtypeStruct((b, N_HEADS, s, LANES), BF16),
            jax.ShapeDtypeStruct((b, N_HEADS, s, LANES), BF16),
            jax.ShapeDtypeStruct((b, N_HEADS * VT_ROWS, s), BF16),
            jax.ShapeDtypeStruct((b, s, POOL_DIM), F32),
        ],
        scratch_shapes=[pltpu.VMEM((SUBLANES, LANES), F32)],
        compiler_params=pltpu.CompilerParams(
            dimension_semantics=("arbitrary", "arbitrary"), vmem_limit_bytes=VMEM_LIMIT),
        name="even_in",
    )(x, g, wqk, wvt, wu, wf, bf, qg, kg, tril, place)


def _attn_kernel(q_ref, k_ref, vt_ref, o_ref, sa_ref, sb_ref, m_ref, mb_ref, acc_ref, *, bq):
    bk = bq // KEY_SPLIT
    i = pl.program_id(2)
    chunks = [slice(c, c + MXU_WIDTH) for c in range(0, bq, MXU_WIDTH)]

    outs = []
    for hh in range(2):
        def fill(s_ref, mb_row, blk_idx, cols, rows=bk, causal_shift=None, hh=hh):
            start = pl.multiple_of(blk_idx * bk, bk)
            s = _dot_nt(k_ref[0, hh, pl.ds(start, rows), :], q_ref[0, hh, cols, :])
            if causal_shift is not None:
                key = lax.broadcasted_iota(jnp.int32, s.shape, 0) + causal_shift
                s = jnp.where(key <= lax.broadcasted_iota(jnp.int32, s.shape, 1), s, NEG_BIG)
            s_ref[0:rows, cols] = s
            mb_ref[mb_row:mb_row + 1, cols] = jnp.max(s, axis=0, keepdims=True)

        def drain(s_ref, mb_row, blk_idx, cols, rows=bk, hh=hh):
            m = m_ref[:, cols]
            m_new = jnp.maximum(m, mb_ref[mb_row:mb_row + 1, cols])
            m_ref[:, cols] = m_new
            p = jnp.exp2(s_ref[0:rows, cols] - m_new).astype(BF16)
            start = pl.multiple_of(blk_idx * bk, bk)
            vb = vt_ref[0, hh * VT_ROWS:(hh + 1) * VT_ROWS, pl.ds(start, rows)]
            acc_ref[:, cols] = jnp.exp2(m - m_new) * acc_ref[:, cols] + _dot(vb, p)

        def step(jj):
            for cols in chunks:
                fill(sb_ref, 1, 2 * jj + 1, cols)
                drain(sa_ref, 0, 2 * jj, cols)
            for cols in chunks:
                fill(sa_ref, 0, 2 * jj + 2, cols)
                drain(sb_ref, 1, 2 * jj + 1, cols)

        def body(jj, carry):
            step(2 * jj)
            step(2 * jj + 1)
            return carry

        m_ref[...] = jnp.full((1, bq), NEG_BIG, F32)
        acc_ref[...] = jnp.zeros((VT_ROWS, bq), F32)

        @pl.when(i > 0)
        def _():
            for cols in chunks:
                fill(sa_ref, 0, 0, cols)

        lax.fori_loop(0, i // 2, body, 0)

        @pl.when(i % 2 == 1)
        def _():
            step(i - 1)

        for d in range(KEY_SPLIT):
            tiles = []
            for cols in chunks:
                visible = min(bk, cols.stop - d * bk)
                if visible > 0:
                    shift = d * bk - cols.start
                    tiles.append((cols, visible, shift if shift + visible - 1 > 0 else None))
            for cols, visible, shift in tiles:
                fill(sa_ref, 0, KEY_SPLIT * i + d, cols, rows=visible, causal_shift=shift)
            for cols, visible, _ in tiles:
                drain(sa_ref, 0, KEY_SPLIT * i + d, cols, rows=visible)
        outs.append(acc_ref[:HEAD_DIM] / acc_ref[HEAD_DIM:HEAD_DIM + 1])
    o_ref[0] = jnp.concatenate(outs, axis=0).T.astype(BF16)


def _fox_attention(q, k, vt, *, bq):
    b, nh, s, _ = q.shape
    kern = functools.partial(_attn_kernel, bq=bq)
    return pl.pallas_call(
        kern,
        grid=(b, nh // 2, s // bq),
        in_specs=[
            pl.BlockSpec((1, 2, bq, LANES), lambda bi, p, i: (bi, p, i, 0)),
            pl.BlockSpec((1, 2, s, LANES), lambda bi, p, i: (bi, p, 0, 0)),
            pl.BlockSpec((1, 2 * VT_ROWS, s), lambda bi, p, i: (bi, p, 0)),
        ],
        out_specs=pl.BlockSpec((1, bq, 2 * HEAD_DIM), lambda bi, p, i: (bi, i, p)),
        out_shape=jax.ShapeDtypeStruct((b, s, ATTN_DIM), BF16),
        scratch_shapes=[
            pltpu.VMEM((bq // KEY_SPLIT, bq), F32), pltpu.VMEM((bq // KEY_SPLIT, bq), F32),
            pltpu.VMEM((1, bq), F32), pltpu.VMEM((2, bq), F32),
            pltpu.VMEM((VT_ROWS, bq), F32)],
        compiler_params=pltpu.CompilerParams(
            dimension_semantics=("arbitrary", "arbitrary", "arbitrary"),
            vmem_limit_bytes=VMEM_LIMIT),
        name="fox_attention",
    )(q, k, vt)


def _even_out_kernel(a_ref, u_ref, x_ref, wpool_ref, pscale_ref, wout_ref, o_ref, carry_ref, *, tm):
    i = pl.program_id(1)

    @pl.when(i == 0)
    def _():
        carry_ref[...] = jnp.zeros_like(carry_ref)

    u = u_ref[0]
    pos = (i * tm + 1 + lax.broadcasted_iota(jnp.int32, (tm, 1), 0)).astype(F32)
    parts = [a_ref[0]]
    for g, w in enumerate(POOL_WINDOWS):
        cols = slice(g * POOL_GROUP_DIM, (g + 1) * POOL_GROUP_DIM)
        ug = u[:, cols]
        run = jnp.concatenate([carry_ref[:, cols], ug], axis=0)
        sh = 1
        while sh < w:
            run = run + pltpu.roll(run, sh, axis=0)
            sh *= 2
        pooled = run[POOL_HALO:] / jnp.minimum(pos, float(w))
        mixed = (pooled - ug).astype(BF16)
        parts.append((_dot(mixed, wpool_ref[g]) * pscale_ref[:, cols]).astype(BF16))
    carry_ref[...] = u[tm - POOL_HALO:, :]
    cat = jnp.concatenate(parts, axis=1)
    o_ref[0] = x_ref[0] + _dot(cat, wout_ref[...])


def _even_out(a, u, x, wpool, pscale, wout, *, tm):
    b, s, d = x.shape
    kern = functools.partial(_even_out_kernel, tm=tm)
    return pl.pallas_call(
        kern,
        grid=(b, s // tm),
        in_specs=[
            pl.BlockSpec((1, tm, ATTN_DIM), lambda bi, i: (bi, i, 0)),
            pl.BlockSpec((1, tm, POOL_DIM), lambda bi, i: (bi, i, 0)),
            pl.BlockSpec((1, tm, d), lambda bi, i: (bi, i, 0)),
            _const_spec(wpool.shape), _const_spec(pscale.shape), _const_spec(wout.shape),
        ],
        out_specs=pl.BlockSpec((1, tm, d), lambda bi, i: (bi, i, 0)),
        out_shape=jax.ShapeDtypeStruct((b, s, d), F32),
        scratch_shapes=[pltpu.VMEM((POOL_HALO, POOL_DIM), F32)],
        compiler_params=pltpu.CompilerParams(
            dimension_semantics=("arbitrary", "arbitrary"), vmem_limit_bytes=VMEM_LIMIT),
        name="even_out",
    )(a, u, x, wpool, pscale, wout)


def _conformer_kernel(x_ref, g_ref, w1_ref, dww_ref, dwb_ref, lng_ref, lnb_ref, w2_ref, o_ref,
                      ext_ref, y_ref, *, tm):
    i = pl.program_id(1)

    @pl.when(i == 0)
    def _():
        ext_ref[0:CONV_HALO, :] = jnp.zeros((CONV_HALO, D_MODEL), F32)

    @pl.when(i > 0)
    def _():
        ext_ref[0:CONV_HALO, :] = ext_ref[tm:tm + CONV_HALO, :]

    x = x_ref[0]
    hb = _rms_norm(x, g_ref[...]).astype(BF16)

    def conv_chunk(col):
        cols = slice(col, col + LANES)
        e = ext_ref[:, cols]
        acc = jnp.broadcast_to(dwb_ref[:, cols], (tm, LANES))
        for r in range(SUBLANES):
            er = e if r == 0 else pltpu.roll(e, r, axis=0)
            for a in range(CONV_HALO // SUBLANES):
                shift = SUBLANES * a + r
                if shift >= CONV_WIDTH:
                    continue
                tap = CONV_WIDTH - 1 - shift
                lo = CONV_HALO - SUBLANES * a
                acc = acc + dww_ref[tap:tap + 1, cols] * er[lo:lo + tm]
        y_ref[:, cols] = acc

    for col in range(0, D_MODEL, MXU_WIDTH):
        a = _dot(hb, w1_ref[:, col:col + MXU_WIDTH])
        gate = _dot(hb, w1_ref[:, D_MODEL + col:D_MODEL + col + MXU_WIDTH])
        ext_ref[CONV_HALO:, col:col + MXU_WIDTH] = a * jax.nn.sigmoid(gate)
        for sub in range(0, MXU_WIDTH, LANES):
            conv_chunk(col + sub)

    y = y_ref[...]
    mu = jnp.mean(y, axis=-1, keepdims=True)
    yc = y - mu
    var = jnp.mean(yc * yc, axis=-1, keepdims=True)
    z = yc * lax.rsqrt(var + EPS) * lng_ref[...] + lnb_ref[...]
    z = z * jax.nn.sigmoid(z)
    o_ref[0] = x + _dot(z.astype(BF16), w2_ref[...])


def _conformer(x, g, w1, dww, dwb, lng, lnb, w2, *, tm):
    b, s, d = x.shape
    kern = functools.partial(_conformer_kernel, tm=tm)
    return pl.pallas_call(
        kern,
        grid=(b, s // tm),
        in_specs=[
            pl.BlockSpec((1, tm, d), lambda bi, i: (bi, i, 0)),
            _const_spec(g.shape), _const_spec(w1.shape), _const_spec(dww.shape),
            _const_spec(dwb.shape), _const_spec(lng.shape), _const_spec(lnb.shape),
            _const_spec(w2.shape),
        ],
        out_specs=pl.BlockSpec((1, tm, d), lambda bi, i: (bi, i, 0)),
        out_shape=jax.ShapeDtypeStruct((b, s, d), F32),
        scratch_shapes=[pltpu.VMEM((CONV_HALO + tm, D_MODEL), F32), pltpu.VMEM((tm, D_MODEL), F32)],
        compiler_params=pltpu.CompilerParams(
            dimension_semantics=("arbitrary", "arbitrary"), vmem_limit_bytes=VMEM_LIMIT),
        name="conformer",
    )(x, g, w1, dww, dwb, lng, lnb, w2)


def _conv_ffn_kernel(x_ref, g_ref, wup_ref, cw_ref, cb_ref, wdown_ref, o_ref, act_ref, carry_ref,
                     *, tm, chunk):
    @pl.when(pl.program_id(1) == 0)
    def _():
        carry_ref[...] = jnp.zeros_like(carry_ref)

    x = x_ref[0]
    hb = _rms_norm(x, g_ref[...]).astype(BF16)

    def conv_cols(col):
        cols = slice(col, col + chunk)
        up = _dot(hb, wup_ref[:, cols])
        ext = jnp.concatenate([carry_ref[:, cols], up], axis=0)
        carry_ref[:, cols] = up[tm - SUBLANES:, :]
        y = cb_ref[:, cols] + cw_ref[2:3, cols] * up
        y = y + cw_ref[1:2, cols] * pltpu.roll(ext, 1, axis=0)[SUBLANES:]
        y = y + cw_ref[0:1, cols] * pltpu.roll(ext, 2, axis=0)[SUBLANES:]
        return y

    for j in range(D_FF // chunk):
        gate = conv_cols(j * chunk)
        val = conv_cols(D_FF + j * chunk)
        act_ref[:, j * chunk:(j + 1) * chunk] = (gate * jax.nn.sigmoid(gate) * val).astype(BF16)

    o_ref[0] = x + _dot(act_ref[...], wdown_ref[...])


def _conv_ffn(x, g, wup, cw, cb, wdown, *, tm, chunk):
    b, s, d = x.shape
    kern = functools.partial(_conv_ffn_kernel, tm=tm, chunk=chunk)
    return pl.pallas_call(
        kern,
        grid=(b, s // tm),
        in_specs=[
            pl.BlockSpec((1, tm, d), lambda bi, i: (bi, i, 0)),
            _const_spec(g.shape), _const_spec(wup.shape), _const_spec(cw.shape),
            _const_spec(cb.shape), _const_spec(wdown.shape),
        ],
        out_specs=pl.BlockSpec((1, tm, d), lambda bi, i: (bi, i, 0)),
        out_shape=jax.ShapeDtypeStruct((b, s, d), F32),
        scratch_shapes=[pltpu.VMEM((tm, D_FF), BF16), pltpu.VMEM((SUBLANES, 2 * D_FF), F32)],
        compiler_params=pltpu.CompilerParams(
            dimension_semantics=("arbitrary", "arbitrary"), vmem_limit_bytes=VMEM_LIMIT),
        name="conv_ffn",
    )(x, g, wup, cw, cb, wdown)


def _pad_rows(a, rows):
    return jnp.pad(a, ((0, rows - a.shape[0]), (0, 0)))


def kernel(x, even_norm_g, even_w_in, even_b_f, even_q_norm_g, even_k_norm_g, even_w_pool,
           even_pool_scale, even_w_out, odd_norm_g, odd_w_pw1, odd_dw_w, odd_dw_b, odd_ln_g,
           odd_ln_b, odd_w_pw2, ffn_norm_g, ffn_w_up, ffn_conv_w, ffn_conv_b, ffn_w_down):
    depth = ffn_norm_g.shape[0]
    seq = x.shape[1]
    tm = min(512, seq)
    blk = min(1024, seq)
    tril = jnp.tril(jnp.ones((tm, tm), BF16))
    place = _gate_placement()
    row = lambda v: v.reshape(1, -1).astype(F32)

    for layer in range(depth):
        i = layer // 2
        if layer % 2 == 0:
            w_in = even_w_in[i]
            o = 2 * ATTN_DIM
            wqk = w_in[:, :o].astype(BF16)
            wvt = w_in[:, o:o + ATTN_DIM].T.astype(BF16)
            o += ATTN_DIM
            wf = jnp.pad(jnp.tile(w_in[:, o:o + N_HEADS], (1, 3)),
                         ((0, 0), (0, LANES - 3 * N_HEADS))).astype(BF16)
            bf = jnp.pad(jnp.tile(even_b_f[i], 3), (0, LANES - 3 * N_HEADS)).reshape(1, LANES).astype(F32)
            o += N_HEADS
            wu = w_in[:, o:o + POOL_DIM].astype(BF16)
            qg = row(jnp.tile(even_q_norm_g[i], 2))
            kg = row(jnp.tile(even_k_norm_g[i], 2))
            q, k, vt, u = _even_in(x, row(even_norm_g[i]), wqk, wvt, wu, wf, bf, qg, kg, tril, place,
                                   tm=tm)
            a = _fox_attention(q, k, vt, bq=blk)
            x = _even_out(a, u, x, even_w_pool[i].astype(BF16), row(even_pool_scale[i]),
                          even_w_out[i].astype(BF16), tm=tm)
        else:
            x = _conformer(x, row(odd_norm_g[i]), odd_w_pw1[i].astype(BF16),
                           _pad_rows(odd_dw_w[i].astype(F32), CONV_HALO), row(odd_dw_b[i]),
                           row(odd_ln_g[i]), row(odd_ln_b[i]), odd_w_pw2[i].astype(BF16), tm=tm)
        x = _conv_ffn(x, row(ffn_norm_g[layer]), ffn_w_up[layer].astype(BF16),
                      _pad_rows(ffn_conv_w[layer].astype(F32), SUBLANES), row(ffn_conv_b[layer]),
                      ffn_w_down[layer].astype(BF16), tm=tm, chunk=MXU_WIDTH)
    return x
```

```python
import functools
import math

import jax
import jax.numpy as jnp
import numpy as np
from jax import lax
from jax.experimental import pallas as pl
from jax.experimental.pallas import tpu as pltpu

D_MODEL = 1024
N_HEADS = 8
HEAD_DIM = 64
ATTN_DIM = N_HEADS * HEAD_DIM
POOL_WINDOWS = (2, 4, 8, 16)
POOL_GROUP_DIM = 128
POOL_DIM = len(POOL_WINDOWS) * POOL_GROUP_DIM
CONV_WIDTH = 31
D_FF = 2816
FFN_CONV_WIDTH = 3
EPS = 1e-6

LANES = 128
SUBLANES = 8
MXU_WIDTH = 256
POOL_HALO = 16
CONV_HALO = 32
NEG_BIG = -1e30
LOG2E = math.log2(math.e)
BF16_SUBLANES = 16
VT_ROWS = HEAD_DIM + BF16_SUBLANES
KEY_SPLIT = 2
VMEM_LIMIT = 56 * 1024 * 1024

F32 = jnp.float32
BF16 = jnp.bfloat16


def _rms_norm(x, g):
    ms = jnp.mean(x * x, axis=-1, keepdims=True)
    return x * lax.rsqrt(ms + EPS) * g


def _dot(a, b):
    return jnp.dot(a, b, preferred_element_type=F32)


def _dot_nt(a, b):
    return lax.dot_general(a, b, (((1,), (1,)), ((), ())), preferred_element_type=F32)


def _split3(c):
    hi = c.astype(BF16).astype(F32)
    r = c - hi
    mid = r.astype(BF16).astype(F32)
    lo = (r - mid).astype(BF16).astype(F32)
    return hi, mid, lo


def _const_spec(shape):
    zeros = (0,) * len(shape)
    return pl.BlockSpec(shape, lambda *_: zeros, pipeline_mode=pl.Buffered(1))


def _even_in_kernel(x_ref, g_ref, wqk_ref, wvt_ref, wu_ref, wf_ref, bf_ref, qg_ref, kg_ref,
                    tril_ref, place_ref, q_ref, k_ref, vt_ref, u_ref, carry_ref, *, tm):
    @pl.when(pl.program_id(1) == 0)
    def _():
        carry_ref[...] = jnp.zeros_like(carry_ref)

    hb = _rms_norm(x_ref[0], g_ref[...]).astype(BF16)
    qk = _dot(hb, wqk_ref[...])
    vt = _dot_nt(wvt_ref[...], hb).astype(BF16)
    for head in range(N_HEADS):
        vt_ref[0, head * VT_ROWS:head * VT_ROWS + HEAD_DIM] = vt[head * HEAD_DIM:(head + 1) * HEAD_DIM]
        vt_ref[0, head * VT_ROWS + HEAD_DIM:(head + 1) * VT_ROWS] = jnp.ones(
            (VT_ROWS - HEAD_DIM, tm), BF16)
    u_ref[0] = _dot(hb, wu_ref[...])

    fl = _dot(hb, wf_ref[...]) + bf_ref[...]
    logf = jnp.minimum(fl, 0.0) - jnp.log1p(jnp.exp(-jnp.abs(fl)))
    tril = tril_ref[...]
    c = carry_ref[0:1, :]
    for term in _split3(logf):
        c = c + _dot(tril, term.astype(BF16))
    carry_ref[0:1, :] = c[tm - 1:tm, :]
    c = c * LOG2E

    lane = lax.broadcasted_iota(jnp.int32, (tm, LANES), 1)
    hi, mid, lo = _split3(c)
    c3 = jnp.where(lane < N_HEADS, hi, jnp.where(lane < 2 * N_HEADS, mid,
         jnp.where(lane < 3 * N_HEADS, lo, jnp.where(lane == 3 * N_HEADS, 1.0, 0.0))))
    aug_qk = _dot(c3.astype(BF16), place_ref[...])

    low = lane < HEAD_DIM
    inv_dh = 1.0 / HEAD_DIM
    scale = LOG2E / math.sqrt(HEAD_DIM)
    for pair in range(N_HEADS // 2):
        for which, (gain_ref, out_ref) in enumerate(((qg_ref, q_ref), (kg_ref, k_ref))):
            col = which * ATTN_DIM + pair * LANES
            slab = qk[:, col:col + LANES]
            sq = slab * slab
            ss_lo = jnp.sum(jnp.where(low, sq, 0.0), axis=-1, keepdims=True)
            ss_hi = jnp.sum(jnp.where(low, 0.0, sq), axis=-1, keepdims=True)
            inv = jnp.where(low, lax.rsqrt(ss_lo * inv_dh + EPS), lax.rsqrt(ss_hi * inv_dh + EPS))
            normed = slab * inv * gain_ref[...]
            if which == 0:
                normed = normed * scale
            aug = aug_qk[:, col:col + LANES]
            out_ref[0, 2 * pair] = jnp.where(low, normed, aug).astype(BF16)
            out_ref[0, 2 * pair + 1] = jnp.where(low, aug, normed).astype(BF16)


def _gate_placement():
    place = np.zeros((LANES, 2 * ATTN_DIM), np.float32)
    for head in range(N_HEADS):
        base = (head // 2) * LANES + (HEAD_DIM if head % 2 == 0 else 0)
        for term in range(3):
            place[term * N_HEADS + head, base + term] = 1.0
            place[3 * N_HEADS, base + 3 + term] = 1.0
            place[3 * N_HEADS, ATTN_DIM + base + term] = 1.0
            place[term * N_HEADS + head, ATTN_DIM + base + 3 + term] = -1.0
    return jnp.asarray(place, BF16)


def _even_in(x, g, wqk, wvt, wu, wf, bf, qg, kg, tril, place, *, tm):
    b, s, d = x.shape
    kern = functools.partial(_even_in_kernel, tm=tm)
    return pl.pallas_call(
        kern,
        grid=(b, s // tm),
        in_specs=[
            pl.BlockSpec((1, tm, d), lambda bi, i: (bi, i, 0)),
            _const_spec(g.shape), _const_spec(wqk.shape), _const_spec(wvt.shape),
            _const_spec(wu.shape), _const_spec(wf.shape), _const_spec(bf.shape),
            _const_spec(qg.shape), _const_spec(kg.shape), _const_spec(tril.shape),
            _const_spec(place.shape),
        ],
        out_specs=[
            pl.BlockSpec((1, N_HEADS, tm, LANES), lambda bi, i: (bi, 0, i, 0)),
            pl.BlockSpec((1, N_HEADS, tm, LANES), lambda bi, i: (bi, 0, i, 0)),
            pl.BlockSpec((1, N_HEADS * VT_ROWS, tm), lambda bi, i: (bi, 0, i)),
            pl.BlockSpec((1, tm, POOL_DIM), lambda bi, i: (bi, i, 0)),
        ],
        out_shape=[
            jax.ShapeDtypeStruct((b, N_HEADS, s, LANES), BF16),
            jax.ShapeDtypeStruct((b, N_HEADS, s, LANES), BF16),
            jax.ShapeDtypeStruct((b, N_HEADS * VT_ROWS, s), BF16),
            jax.ShapeDtypeStruct((b, s, POOL_DIM), F32),
        ],
        scratch_shapes=[pltpu.VMEM((SUBLANES, LANES), F32)],
        compiler_params=pltpu.CompilerParams(
            dimension_semantics=("arbitrary", "arbitrary"), vmem_limit_bytes=VMEM_LIMIT),
        name="even_in",
    )(x, g, wqk, wvt, wu, wf, bf, qg, kg, tril, place)


def _attn_kernel(q_ref, k_ref, vt_ref, o_ref, sa_ref, sb_ref, m_ref, mb_ref, acc_ref, *, bq):
    bk = bq // KEY_SPLIT
    i = pl.program_id(2)
    chunks = [slice(c, c + MXU_WIDTH) for c in range(0, bq, MXU_WIDTH)]

    def diagonal_tile(d, cols):
        visible = min(bk, cols.stop - d * bk)
        shift = d * bk - cols.start
        return max(visible, 0), (shift if shift + visible - 1 > 0 else None)

    outs = []
    for hh in range(2):
        def fill(s_ref, mb_row, blk_idx, cols, rows=bk, causal_shift=None, hh=hh):
            start = pl.multiple_of(blk_idx * bk, bk)
            s = _dot_nt(k_ref[0, hh, pl.ds(start, rows), :], q_ref[0, hh, cols, :])
            if causal_shift is not None:
                key = lax.broadcasted_iota(jnp.int32, s.shape, 0) + causal_shift
                s = jnp.where(key <= lax.broadcasted_iota(jnp.int32, s.shape, 1), s, NEG_BIG)
            s_ref[0:rows, cols] = s
            mb_ref[mb_row:mb_row + 1, cols] = jnp.max(s, axis=0, keepdims=True)

        def drain(s_ref, mb_row, blk_idx, cols, rows=bk, hh=hh):
            m = m_ref[:, cols]
            m_new = jnp.maximum(m, mb_ref[mb_row:mb_row + 1, cols])
            m_ref[:, cols] = m_new
            p = jnp.exp2(s_ref[0:rows, cols] - m_new).astype(BF16)
            start = pl.multiple_of(blk_idx * bk, bk)
            vb = vt_ref[0, hh * VT_ROWS:(hh + 1) * VT_ROWS, pl.ds(start, rows)]
            acc_ref[:, cols] = jnp.exp2(m - m_new) * acc_ref[:, cols] + _dot(vb, p)

        def fill_diagonal(d, s_ref, mb_row, cols):
            rows, shift = diagonal_tile(d, cols)
            if rows:
                fill(s_ref, mb_row, KEY_SPLIT * i + d, cols, rows=rows, causal_shift=shift)

        def drain_diagonal(d, s_ref, mb_row, cols):
            rows, _ = diagonal_tile(d, cols)
            if rows:
                drain(s_ref, mb_row, KEY_SPLIT * i + d, cols, rows=rows)

        def step(jj, last=False):
            for cols in chunks:
                fill(sb_ref, 1, 2 * jj + 1, cols)
                drain(sa_ref, 0, 2 * jj, cols)
            for cols in chunks:
                if last:
                    fill_diagonal(0, sa_ref, 0, cols)
                else:
                    fill(sa_ref, 0, 2 * jj + 2, cols)
                drain(sb_ref, 1, 2 * jj + 1, cols)

        def body(jj, carry):
            step(2 * jj)
            step(2 * jj + 1)
            return carry

        m_ref[...] = jnp.full((1, bq), NEG_BIG, F32)
        acc_ref[...] = jnp.zeros((VT_ROWS, bq), F32)

        @pl.when(i == 0)
        def _():
            for cols in chunks:
                fill_diagonal(0, sa_ref, 0, cols)

        @pl.when(i > 0)
        def _():
            for cols in chunks:
                fill(sa_ref, 0, 0, cols)

        plain = jnp.maximum(i - 1, 0)
        lax.fori_loop(0, plain // 2, body, 0)

        @pl.when(plain % 2 == 1)
        def _():
            step(plain - 1)

        @pl.when(i > 0)
        def _():
            step(i - 1, last=True)

        for cols in chunks:
            fill_diagonal(1, sb_ref, 1, cols)
            drain_diagonal(0, sa_ref, 0, cols)
        for cols in chunks:
            drain_diagonal(1, sb_ref, 1, cols)
        outs.append(acc_ref[:HEAD_DIM] / acc_ref[HEAD_DIM:HEAD_DIM + 1])
    o_ref[0] = jnp.concatenate(outs, axis=0).T.astype(BF16)


def _fox_attention(q, k, vt, *, bq):
    b, nh, s, _ = q.shape
    kern = functools.partial(_attn_kernel, bq=bq)
    return pl.pallas_call(
        kern,
        grid=(b, nh // 2, s // bq),
        in_specs=[
            pl.BlockSpec((1, 2, bq, LANES), lambda bi, p, i: (bi, p, i, 0)),
            pl.BlockSpec((1, 2, s, LANES), lambda bi, p, i: (bi, p, 0, 0)),
            pl.BlockSpec((1, 2 * VT_ROWS, s), lambda bi, p, i: (bi, p, 0)),
        ],
        out_specs=pl.BlockSpec((1, bq, 2 * HEAD_DIM), lambda bi, p, i: (bi, i, p)),
        out_shape=jax.ShapeDtypeStruct((b, s, ATTN_DIM), BF16),
        scratch_shapes=[
            pltpu.VMEM((bq // KEY_SPLIT, bq), F32), pltpu.VMEM((bq // KEY_SPLIT, bq), F32),
            pltpu.VMEM((1, bq), F32), pltpu.VMEM((2, bq), F32),
            pltpu.VMEM((VT_ROWS, bq), F32)],
        compiler_params=pltpu.CompilerParams(
            dimension_semantics=("arbitrary", "arbitrary", "arbitrary"),
            vmem_limit_bytes=VMEM_LIMIT),
        name="fox_attention",
    )(q, k, vt)


def _even_out_kernel(a_ref, u_ref, x_ref, wpool_ref, pscale_ref, wout_ref, o_ref, carry_ref, *, tm):
    i = pl.program_id(1)

    @pl.when(i == 0)
    def _():
        carry_ref[...] = jnp.zeros_like(carry_ref)

    u = u_ref[0]
    pos = (i * tm + 1 + lax.broadcasted_iota(jnp.int32, (tm, 1), 0)).astype(F32)
    parts = [a_ref[0]]
    for g, w in enumerate(POOL_WINDOWS):
        cols = slice(g * POOL_GROUP_DIM, (g + 1) * POOL_GROUP_DIM)
        ug = u[:, cols]
        run = jnp.concatenate([carry_ref[:, cols], ug], axis=0)
        sh = 1
        while sh < w:
            run = run + pltpu.roll(run, sh, axis=0)
            sh *= 2
        pooled = run[POOL_HALO:] / jnp.minimum(pos, float(w))
        mixed = (pooled - ug).astype(BF16)
        parts.append((_dot(mixed, wpool_ref[g]) * pscale_ref[:, cols]).astype(BF16))
    carry_ref[...] = u[tm - POOL_HALO:, :]
    cat = jnp.concatenate(parts, axis=1)
    o_ref[0] = x_ref[0] + _dot(cat, wout_ref[...])


def _even_out(a, u, x, wpool, pscale, wout, *, tm):
    b, s, d = x.shape
    kern = functools.partial(_even_out_kernel, tm=tm)
    return pl.pallas_call(
        kern,
        grid=(b, s // tm),
        in_specs=[
            pl.BlockSpec((1, tm, ATTN_DIM), lambda bi, i: (bi, i, 0)),
            pl.BlockSpec((1, tm, POOL_DIM), lambda bi, i: (bi, i, 0)),
            pl.BlockSpec((1, tm, d), lambda bi, i: (bi, i, 0)),
            _const_spec(wpool.shape), _const_spec(pscale.shape), _const_spec(wout.shape),
        ],
        out_specs=pl.BlockSpec((1, tm, d), lambda bi, i: (bi, i, 0)),
        out_shape=jax.ShapeDtypeStruct((b, s, d), F32),
        scratch_shapes=[pltpu.VMEM((POOL_HALO, POOL_DIM), F32)],
        compiler_params=pltpu.CompilerParams(
            dimension_semantics=("arbitrary", "arbitrary"), vmem_limit_bytes=VMEM_LIMIT),
        name="even_out",
    )(a, u, x, wpool, pscale, wout)


def _conformer_kernel(x_ref, g_ref, w1_ref, dww_ref, dwb_ref, lng_ref, lnb_ref, w2_ref, o_ref,
                      ext_ref, y_ref, *, tm):
    i = pl.program_id(1)

    @pl.when(i == 0)
    def _():
        ext_ref[0:CONV_HALO, :] = jnp.zeros((CONV_HALO, D_MODEL), F32)

    @pl.when(i > 0)
    def _():
        ext_ref[0:CONV_HALO, :] = ext_ref[tm:tm + CONV_HALO, :]

    x = x_ref[0]
    hb = _rms_norm(x, g_ref[...]).astype(BF16)

    def conv_chunk(col):
        cols = slice(col, col + LANES)
        e = ext_ref[:, cols]
        acc = jnp.broadcast_to(dwb_ref[:, cols], (tm, LANES))
        for r in range(SUBLANES):
            er = e if r == 0 else pltpu.roll(e, r, axis=0)
            for a in range(CONV_HALO // SUBLANES):
                shift = SUBLANES * a + r
                if shift >= CONV_WIDTH:
                    continue
                tap = CONV_WIDTH - 1 - shift
                lo = CONV_HALO - SUBLANES * a
                acc = acc + dww_ref[tap:tap + 1, cols] * er[lo:lo + tm]
        y_ref[:, cols] = acc

    for col in range(0, D_MODEL, MXU_WIDTH):
        a = _dot(hb, w1_ref[:, col:col + MXU_WIDTH])
        gate = _dot(hb, w1_ref[:, D_MODEL + col:D_MODEL + col + MXU_WIDTH])
        ext_ref[CONV_HALO:, col:col + MXU_WIDTH] = a * jax.nn.sigmoid(gate)
        for sub in range(0, MXU_WIDTH, LANES):
            conv_chunk(col + sub)

    y = y_ref[...]
    mu = jnp.mean(y, axis=-1, keepdims=True)
    yc = y - mu
    var = jnp.mean(yc * yc, axis=-1, keepdims=True)
    z = yc * lax.rsqrt(var + EPS) * lng_ref[...] + lnb_ref[...]
    z = z * jax.nn.sigmoid(z)
    o_ref[0] = x + _dot(z.astype(BF16), w2_ref[...])


def _conformer(x, g, w1, dww, dwb, lng, lnb, w2, *, tm):
    b, s, d = x.shape
    kern = functools.partial(_conformer_kernel, tm=tm)
    return pl.pallas_call(
        kern,
        grid=(b, s // tm),
        in_specs=[
            pl.BlockSpec((1, tm, d), lambda bi, i: (bi, i, 0)),
            _const_spec(g.shape), _const_spec(w1.shape), _const_spec(dww.shape),
            _const_spec(dwb.shape), _const_spec(lng.shape), _const_spec(lnb.shape),
            _const_spec(w2.shape),
        ],
        out_specs=pl.BlockSpec((1, tm, d), lambda bi, i: (bi, i, 0)),
        out_shape=jax.ShapeDtypeStruct((b, s, d), F32),
        scratch_shapes=[pltpu.VMEM((CONV_HALO + tm, D_MODEL), F32), pltpu.VMEM((tm, D_MODEL), F32)],
        compiler_params=pltpu.CompilerParams(
            dimension_semantics=("arbitrary", "arbitrary"), vmem_limit_bytes=VMEM_LIMIT),
        name="conformer",
    )(x, g, w1, dww, dwb, lng, lnb, w2)


def _conv_ffn_kernel(x_ref, g_ref, wup_ref, cw_ref, cb_ref, wdown_ref, o_ref, act_ref, carry_ref,
                     *, tm, chunk):
    @pl.when(pl.program_id(1) == 0)
    def _():
        carry_ref[...] = jnp.zeros_like(carry_ref)

    x = x_ref[0]
    hb = _rms_norm(x, g_ref[...]).astype(BF16)

    def conv_cols(col):
        cols = slice(col, col + chunk)
        up = _dot(hb, wup_ref[:, cols])
        ext = jnp.concatenate([carry_ref[:, cols], up], axis=0)
        carry_ref[:, cols] = up[tm - SUBLANES:, :]
        y = cb_ref[:, cols] + cw_ref[2:3, cols] * up
        y = y + cw_ref[1:2, cols] * pltpu.roll(ext, 1, axis=0)[SUBLANES:]
        y = y + cw_ref[0:1, cols] * pltpu.roll(ext, 2, axis=0)[SUBLANES:]
        return y

    for j in range(D_FF // chunk):
        gate = conv_cols(j * chunk)
        val = conv_cols(D_FF + j * chunk)
        act_ref[:, j * chunk:(j + 1) * chunk] = (gate * jax.nn.sigmoid(gate) * val).astype(BF16)

    o_ref[0] = x + _dot(act_ref[...], wdown_ref[...])


def _conv_ffn(x, g, wup, cw, cb, wdown, *, tm, chunk):
    b, s, d = x.shape
    kern = functools.partial(_conv_ffn_kernel, tm=tm, chunk=chunk)
    return pl.pallas_call(
        kern,
        grid=(b, s // tm),
        in_specs=[
            pl.BlockSpec((1, tm, d), lambda bi, i: (bi, i, 0)),
            _const_spec(g.shape), _const_spec(wup.shape), _const_spec(cw.shape),
            _const_spec(cb.shape), _const_spec(wdown.shape),
        ],
        out_specs=pl.BlockSpec((1, tm, d), lambda bi, i: (bi, i, 0)),
        out_shape=jax.ShapeDtypeStruct((b, s, d), F32),
        scratch_shapes=[pltpu.VMEM((tm, D_FF), BF16), pltpu.VMEM((SUBLANES, 2 * D_FF), F32)],
        compiler_params=pltpu.CompilerParams(
            dimension_semantics=("arbitrary", "arbitrary"), vmem_limit_bytes=VMEM_LIMIT),
        name="conv_ffn",
    )(x, g, wup, cw, cb, wdown)


def _pad_rows(a, rows):
    return jnp.pad(a, ((0, rows - a.shape[0]), (0, 0)))


def kernel(x, even_norm_g, even_w_in, even_b_f, even_q_norm_g, even_k_norm_g, even_w_pool,
           even_pool_scale, even_w_out, odd_norm_g, odd_w_pw1, odd_dw_w, odd_dw_b, odd_ln_g,
           odd_ln_b, odd_w_pw2, ffn_norm_g, ffn_w_up, ffn_conv_w, ffn_conv_b, ffn_w_down):
    depth = ffn_norm_g.shape[0]
    seq = x.shape[1]
    tm = min(512, seq)
    blk = min(1024, seq)
    tril = jnp.tril(jnp.ones((tm, tm), BF16))
    place = _gate_placement()
    row = lambda v: v.reshape(1, -1).astype(F32)

    for layer in range(depth):
        i = layer // 2
        if layer % 2 == 0:
            w_in = even_w_in[i]
            o = 2 * ATTN_DIM
            wqk = w_in[:, :o].astype(BF16)
            wvt = w_in[:, o:o + ATTN_DIM].T.astype(BF16)
            o += ATTN_DIM
            wf = jnp.pad(jnp.tile(w_in[:, o:o + N_HEADS], (1, 3)),
                         ((0, 0), (0, LANES - 3 * N_HEADS))).astype(BF16)
            bf = jnp.pad(jnp.tile(even_b_f[i], 3), (0, LANES - 3 * N_HEADS)).reshape(1, LANES).astype(F32)
            o += N_HEADS
            wu = w_in[:, o:o + POOL_DIM].astype(BF16)
            qg = row(jnp.tile(even_q_norm_g[i], 2))
            kg = row(jnp.tile(even_k_norm_g[i], 2))
            q, k, vt, u = _even_in(x, row(even_norm_g[i]), wqk, wvt, wu, wf, bf, qg, kg, tril, place,
                                   tm=tm)
            a = _fox_attention(q, k, vt, bq=blk)
            x = _even_out(a, u, x, even_w_pool[i].astype(BF16), row(even_pool_scale[i]),
                          even_w_out[i].astype(BF16), tm=tm)
        else:
            x = _conformer(x, row(odd_norm_g[i]), odd_w_pw1[i].astype(BF16),
                           _pad_rows(odd_dw_w[i].astype(F32), CONV_HALO), row(odd_dw_b[i]),
                           row(odd_ln_g[i]), row(odd_ln_b[i]), odd_w_pw2[i].astype(BF16), tm=tm)
        x = _conv_ffn(x, row(ffn_norm_g[layer]), ffn_w_up[layer].astype(BF16),
                      _pad_rows(ffn_conv_w[layer].astype(F32), SUBLANES), row(ffn_conv_b[layer]),
                      ffn_w_down[layer].astype(BF16), tm=tm, chunk=MXU_WIDTH)
    return x
```

```python
import functools
import math

import jax
import jax.numpy as jnp
import numpy as np
from jax import lax
from jax.experimental import pallas as pl
from jax.experimental.pallas import tpu as pltpu

D_MODEL = 1024
N_HEADS = 8
HEAD_DIM = 64
ATTN_DIM = N_HEADS * HEAD_DIM
POOL_WINDOWS = (2, 4, 8, 16)
POOL_GROUP_DIM = 128
POOL_DIM = len(POOL_WINDOWS) * POOL_GROUP_DIM
CONV_WIDTH = 31
D_FF = 2816
FFN_CONV_WIDTH = 3
EPS = 1e-6

LANES = 128
SUBLANES = 8
MXU_WIDTH = 256
POOL_HALO = 16
CONV_HALO = 32
NEG_BIG = -1e30
LOG2E = math.log2(math.e)
BF16_SUBLANES = 16
VT_ROWS = HEAD_DIM + BF16_SUBLANES
KEY_SPLIT = 4
VMEM_LIMIT = 56 * 1024 * 1024

F32 = jnp.float32
BF16 = jnp.bfloat16


def _rms_norm(x, g):
    ms = jnp.mean(x * x, axis=-1, keepdims=True)
    return x * lax.rsqrt(ms + EPS) * g


def _dot(a, b):
    return jnp.dot(a, b, preferred_element_type=F32)


def _dot_nt(a, b):
    return lax.dot_general(a, b, (((1,), (1,)), ((), ())), preferred_element_type=F32)


def _split3(c):
    hi = c.astype(BF16).astype(F32)
    r = c - hi
    mid = r.astype(BF16).astype(F32)
    lo = (r - mid).astype(BF16).astype(F32)
    return hi, mid, lo


def _const_spec(shape):
    zeros = (0,) * len(shape)
    return pl.BlockSpec(shape, lambda *_: zeros, pipeline_mode=pl.Buffered(1))


def _even_in_kernel(x_ref, g_ref, wqk_ref, wvt_ref, wu_ref, wf_ref, bf_ref, qg_ref, kg_ref,
                    tril_ref, place_ref, q_ref, k_ref, vt_ref, u_ref, carry_ref, *, tm):
    @pl.when(pl.program_id(1) == 0)
    def _():
        carry_ref[...] = jnp.zeros_like(carry_ref)

    hb = _rms_norm(x_ref[0], g_ref[...]).astype(BF16)
    qk = _dot(hb, wqk_ref[...])
    vt = _dot_nt(wvt_ref[...], hb).astype(BF16)
    for head in range(N_HEADS):
        vt_ref[0, head * VT_ROWS:head * VT_ROWS + HEAD_DIM] = vt[head * HEAD_DIM:(head + 1) * HEAD_DIM]
        vt_ref[0, head * VT_ROWS + HEAD_DIM:(head + 1) * VT_ROWS] = jnp.ones(
            (VT_ROWS - HEAD_DIM, tm), BF16)
    u_ref[0] = _dot(hb, wu_ref[...])

    fl = _dot(hb, wf_ref[...]) + bf_ref[...]
    logf = jnp.minimum(fl, 0.0) - jnp.log1p(jnp.exp(-jnp.abs(fl)))
    tril = tril_ref[...]
    c = carry_ref[0:1, :]
    for term in _split3(logf):
        c = c + _dot(tril, term.astype(BF16))
    carry_ref[0:1, :] = c[tm - 1:tm, :]
    c = c * LOG2E

    lane = lax.broadcasted_iota(jnp.int32, (tm, LANES), 1)
    hi, mid, lo = _split3(c)
    c3 = jnp.where(lane < N_HEADS, hi, jnp.where(lane < 2 * N_HEADS, mid,
         jnp.where(lane < 3 * N_HEADS, lo, jnp.where(lane == 3 * N_HEADS, 1.0, 0.0))))
    aug_qk = _dot(c3.astype(BF16), place_ref[...])

    low = lane < HEAD_DIM
    inv_dh = 1.0 / HEAD_DIM
    scale = LOG2E / math.sqrt(HEAD_DIM)
    for pair in range(N_HEADS // 2):
        for which, (gain_ref, out_ref) in enumerate(((qg_ref, q_ref), (kg_ref, k_ref))):
            col = which * ATTN_DIM + pair * LANES
            slab = qk[:, col:col + LANES]
            sq = slab * slab
            ss_lo = jnp.sum(jnp.where(low, sq, 0.0), axis=-1, keepdims=True)
            ss_hi = jnp.sum(jnp.where(low, 0.0, sq), axis=-1, keepdims=True)
            inv = jnp.where(low, lax.rsqrt(ss_lo * inv_dh + EPS), lax.rsqrt(ss_hi * inv_dh + EPS))
            normed = slab * inv * gain_ref[...]
            if which == 0:
                normed = normed * scale
            aug = aug_qk[:, col:col + LANES]
            out_ref[0, 2 * pair] = jnp.where(low, normed, aug).astype(BF16)
            out_ref[0, 2 * pair + 1] = jnp.where(low, aug, normed).astype(BF16)


def _gate_placement():
    place = np.zeros((LANES, 2 * ATTN_DIM), np.float32)
    for head in range(N_HEADS):
        base = (head // 2) * LANES + (HEAD_DIM if head % 2 == 0 else 0)
        for term in range(3):
            place[term * N_HEADS + head, base + term] = 1.0
            place[3 * N_HEADS, base + 3 + term] = 1.0
            place[3 * N_HEADS, ATTN_DIM + base + term] = 1.0
            place[term * N_HEADS + head, ATTN_DIM + base + 3 + term] = -1.0
    return jnp.asarray(place, BF16)


def _even_in(x, g, wqk, wvt, wu, wf, bf, qg, kg, tril, place, *, tm):
    b, s, d = x.shape
    kern = functools.partial(_even_in_kernel, tm=tm)
    return pl.pallas_call(
        kern,
        grid=(b, s // tm),
        in_specs=[
            pl.BlockSpec((1, tm, d), lambda bi, i: (bi, i, 0)),
            _const_spec(g.shape), _const_spec(wqk.shape), _const_spec(wvt.shape),
            _const_spec(wu.shape), _const_spec(wf.shape), _const_spec(bf.shape),
            _const_spec(qg.shape), _const_spec(kg.shape), _const_spec(tril.shape),
            _const_spec(place.shape),
        ],
        out_specs=[
            pl.BlockSpec((1, N_HEADS, tm, LANES), lambda bi, i: (bi, 0, i, 0)),
            pl.BlockSpec((1, N_HEADS, tm, LANES), lambda bi, i: (bi, 0, i, 0)),
            pl.BlockSpec((1, N_HEADS * VT_ROWS, tm), lambda bi, i: (bi, 0, i)),
            pl.BlockSpec((1, tm, POOL_DIM), lambda bi, i: (bi, i, 0)),
        ],
        out_shape=[
            jax.ShapeDtypeStruct((b, N_HEADS, s, LANES), BF16),
            jax.ShapeDtypeStruct((b, N_HEADS, s, LANES), BF16),
            jax.ShapeDtypeStruct((b, N_HEADS * VT_ROWS, s), BF16),
            jax.ShapeDtypeStruct((b, s, POOL_DIM), F32),
        ],
        scratch_shapes=[pltpu.VMEM((SUBLANES, LANES), F32)],
        compiler_params=pltpu.CompilerParams(
            dimension_semantics=("arbitrary", "arbitrary"), vmem_limit_bytes=VMEM_LIMIT),
        name="even_in",
    )(x, g, wqk, wvt, wu, wf, bf, qg, kg, tril, place)


def _attn_kernel(q_ref, k_ref, vt_ref, o_ref, sa_ref, sb_ref, m_ref, mb_ref, acc_ref, *, bq):
    bk = bq // KEY_SPLIT
    i = pl.program_id(2)
    chunks = [slice(c, c + MXU_WIDTH) for c in range(0, bq, MXU_WIDTH)]

    def diagonal_tile(d, cols):
        visible = min(bk, cols.stop - d * bk)
        shift = d * bk - cols.start
        return max(visible, 0), (shift if shift + visible - 1 > 0 else None)

    outs = []
    for hh in range(2):
        def fill(s_ref, mb_row, blk_idx, cols, rows=bk, causal_shift=None, hh=hh):
            start = pl.multiple_of(blk_idx * bk, bk)
            s = _dot_nt(k_ref[0, hh, pl.ds(start, rows), :], q_ref[0, hh, cols, :])
            if causal_shift is not None:
                key = lax.broadcasted_iota(jnp.int32, s.shape, 0) + causal_shift
                s = jnp.where(key <= lax.broadcasted_iota(jnp.int32, s.shape, 1), s, NEG_BIG)
            s_ref[0:rows, cols] = s
            mb_ref[mb_row:mb_row + 1, cols] = jnp.max(s, axis=0, keepdims=True)

        def drain(s_ref, mb_row, blk_idx, cols, rows=bk, hh=hh):
            m = m_ref[:, cols]
            m_new = jnp.maximum(m, mb_ref[mb_row:mb_row + 1, cols])
            m_ref[:, cols] = m_new
            p = jnp.exp2(s_ref[0:rows, cols] - m_new).astype(BF16)
            start = pl.multiple_of(blk_idx * bk, bk)
            vb = vt_ref[0, hh * VT_ROWS:(hh + 1) * VT_ROWS, pl.ds(start, rows)]
            acc_ref[:, cols] = jnp.exp2(m - m_new) * acc_ref[:, cols] + _dot(vb, p)

        def fill_diagonal(d, s_ref, mb_row, cols):
            rows, shift = diagonal_tile(d, cols)
            if rows:
                fill(s_ref, mb_row, KEY_SPLIT * i + d, cols, rows=rows, causal_shift=shift)

        def drain_diagonal(d, s_ref, mb_row, cols):
            rows, _ = diagonal_tile(d, cols)
            if rows:
                drain(s_ref, mb_row, KEY_SPLIT * i + d, cols, rows=rows)

        def step(jj, last=False):
            for cols in chunks:
                fill(sb_ref, 1, 2 * jj + 1, cols)
                drain(sa_ref, 0, 2 * jj, cols)
            for cols in chunks:
                if last:
                    fill_diagonal(0, sa_ref, 0, cols)
                else:
                    fill(sa_ref, 0, 2 * jj + 2, cols)
                drain(sb_ref, 1, 2 * jj + 1, cols)

        def body(jj, carry):
            step(2 * jj)
            step(2 * jj + 1)
            return carry

        m_ref[...] = jnp.full((1, bq), NEG_BIG, F32)
        acc_ref[...] = jnp.zeros((VT_ROWS, bq), F32)

        @pl.when(i == 0)
        def _():
            for cols in chunks:
                fill_diagonal(0, sa_ref, 0, cols)

        @pl.when(i > 0)
        def _():
            for cols in chunks:
                fill(sa_ref, 0, 0, cols)

        steps = (KEY_SPLIT // 2) * i
        plain = jnp.maximum(steps - 1, 0)
        lax.fori_loop(0, plain // 2, body, 0)

        @pl.when(plain % 2 == 1)
        def _():
            step(plain - 1)

        @pl.when(i > 0)
        def _():
            step(steps - 1, last=True)

        bufs = ((sa_ref, 0), (sb_ref, 1))
        for d in range(1, KEY_SPLIT):
            for cols in chunks:
                fill_diagonal(d, *bufs[d % 2], cols)
                drain_diagonal(d - 1, *bufs[(d - 1) % 2], cols)
        for cols in chunks:
            drain_diagonal(KEY_SPLIT - 1, *bufs[(KEY_SPLIT - 1) % 2], cols)
        outs.append(acc_ref[:HEAD_DIM] / acc_ref[HEAD_DIM:HEAD_DIM + 1])
    o_ref[0] = jnp.concatenate(outs, axis=0).T.astype(BF16)


def _fox_attention(q, k, vt, *, bq):
    b, nh, s, _ = q.shape
    kern = functools.partial(_attn_kernel, bq=bq)
    return pl.pallas_call(
        kern,
        grid=(b, nh // 2, s // bq),
        in_specs=[
            pl.BlockSpec((1, 2, bq, LANES), lambda bi, p, i: (bi, p, i, 0)),
            pl.BlockSpec((1, 2, s, LANES), lambda bi, p, i: (bi, p, 0, 0)),
            pl.BlockSpec((1, 2 * VT_ROWS, s), lambda bi, p, i: (bi, p, 0)),
        ],
        out_specs=pl.BlockSpec((1, bq, 2 * HEAD_DIM), lambda bi, p, i: (bi, i, p)),
        out_shape=jax.ShapeDtypeStruct((b, s, ATTN_DIM), BF16),
        scratch_shapes=[
            pltpu.VMEM((bq // KEY_SPLIT, bq), F32), pltpu.VMEM((bq // KEY_SPLIT, bq), F32),
            pltpu.VMEM((1, bq), F32), pltpu.VMEM((2, bq), F32),
            pltpu.VMEM((VT_ROWS, bq), F32)],
        compiler_params=pltpu.CompilerParams(
            dimension_semantics=("arbitrary", "arbitrary", "arbitrary"),
            vmem_limit_bytes=VMEM_LIMIT),
        name="fox_attention",
    )(q, k, vt)


def _even_out_kernel(a_ref, u_ref, x_ref, wpool_ref, pscale_ref, wout_ref, o_ref, carry_ref, *, tm):
    i = pl.program_id(1)

    @pl.when(i == 0)
    def _():
        carry_ref[...] = jnp.zeros_like(carry_ref)

    u = u_ref[0]
    pos = (i * tm + 1 + lax.broadcasted_iota(jnp.int32, (tm, 1), 0)).astype(F32)
    parts = [a_ref[0]]
    for g, w in enumerate(POOL_WINDOWS):
        cols = slice(g * POOL_GROUP_DIM, (g + 1) * POOL_GROUP_DIM)
        ug = u[:, cols]
        run = jnp.concatenate([carry_ref[:, cols], ug], axis=0)
        sh = 1
        while sh < w:
            run = run + pltpu.roll(run, sh, axis=0)
            sh *= 2
        pooled = run[POOL_HALO:] / jnp.minimum(pos, float(w))
        mixed = (pooled - ug).astype(BF16)
        parts.append((_dot(mixed, wpool_ref[g]) * pscale_ref[:, cols]).astype(BF16))
    carry_ref[...] = u[tm - POOL_HALO:, :]
    cat = jnp.concatenate(parts, axis=1)
    o_ref[0] = x_ref[0] + _dot(cat, wout_ref[...])


def _even_out(a, u, x, wpool, pscale, wout, *, tm):
    b, s, d = x.shape
    kern = functools.partial(_even_out_kernel, tm=tm)
    return pl.pallas_call(
        kern,
        grid=(b, s // tm),
        in_specs=[
            pl.BlockSpec((1, tm, ATTN_DIM), lambda bi, i: (bi, i, 0)),
            pl.BlockSpec((1, tm, POOL_DIM), lambda bi, i: (bi, i, 0)),
            pl.BlockSpec((1, tm, d), lambda bi, i: (bi, i, 0)),
            _const_spec(wpool.shape), _const_spec(pscale.shape), _const_spec(wout.shape),
        ],
        out_specs=pl.BlockSpec((1, tm, d), lambda bi, i: (bi, i, 0)),
        out_shape=jax.ShapeDtypeStruct((b, s, d), F32),
        scratch_shapes=[pltpu.VMEM((POOL_HALO, POOL_DIM), F32)],
        compiler_params=pltpu.CompilerParams(
            dimension_semantics=("arbitrary", "arbitrary"), vmem_limit_bytes=VMEM_LIMIT),
        name="even_out",
    )(a, u, x, wpool, pscale, wout)


def _conformer_kernel(x_ref, g_ref, w1_ref, dww_ref, dwb_ref, lng_ref, lnb_ref, w2_ref, o_ref,
                      ext_ref, y_ref, *, tm):
    i = pl.program_id(1)

    @pl.when(i == 0)
    def _():
        ext_ref[0:CONV_HALO, :] = jnp.zeros((CONV_HALO, D_MODEL), F32)

    @pl.when(i > 0)
    def _():
        ext_ref[0:CONV_HALO, :] = ext_ref[tm:tm + CONV_HALO, :]

    x = x_ref[0]
    hb = _rms_norm(x, g_ref[...]).astype(BF16)

    def conv_chunk(col):
        cols = slice(col, col + LANES)
        e = ext_ref[:, cols]
        acc = jnp.broadcast_to(dwb_ref[:, cols], (tm, LANES))
        for r in range(SUBLANES):
            er = e if r == 0 else pltpu.roll(e, r, axis=0)
            for a in range(CONV_HALO // SUBLANES):
                shift = SUBLANES * a + r
                if shift >= CONV_WIDTH:
                    continue
                tap = CONV_WIDTH - 1 - shift
                lo = CONV_HALO - SUBLANES * a
                acc = acc + dww_ref[tap:tap + 1, cols] * er[lo:lo + tm]
        y_ref[:, cols] = acc

    for col in range(0, D_MODEL, MXU_WIDTH):
        a = _dot(hb, w1_ref[:, col:col + MXU_WIDTH])
        gate = _dot(hb, w1_ref[:, D_MODEL + col:D_MODEL + col + MXU_WIDTH])
        ext_ref[CONV_HALO:, col:col + MXU_WIDTH] = a * jax.nn.sigmoid(gate)
        for sub in range(0, MXU_WIDTH, LANES):
            conv_chunk(col + sub)

    y = y_ref[...]
    mu = jnp.mean(y, axis=-1, keepdims=True)
    yc = y - mu
    var = jnp.mean(yc * yc, axis=-1, keepdims=True)
    z = yc * lax.rsqrt(var + EPS) * lng_ref[...] + lnb_ref[...]
    z = z * jax.nn.sigmoid(z)
    o_ref[0] = x + _dot(z.astype(BF16), w2_ref[...])


def _conformer(x, g, w1, dww, dwb, lng, lnb, w2, *, tm):
    b, s, d = x.shape
    kern = functools.partial(_conformer_kernel, tm=tm)
    return pl.pallas_call(
        kern,
        grid=(b, s // tm),
        in_specs=[
            pl.BlockSpec((1, tm, d), lambda bi, i: (bi, i, 0)),
            _const_spec(g.shape), _const_spec(w1.shape), _const_spec(dww.shape),
            _const_spec(dwb.shape), _const_spec(lng.shape), _const_spec(lnb.shape),
            _const_spec(w2.shape),
        ],
        out_specs=pl.BlockSpec((1, tm, d), lambda bi, i: (bi, i, 0)),
        out_shape=jax.ShapeDtypeStruct((b, s, d), F32),
        scratch_shapes=[pltpu.VMEM((CONV_HALO + tm, D_MODEL), F32), pltpu.VMEM((tm, D_MODEL), F32)],
        compiler_params=pltpu.CompilerParams(
            dimension_semantics=("arbitrary", "arbitrary"), vmem_limit_bytes=VMEM_LIMIT),
        name="conformer",
    )(x, g, w1, dww, dwb, lng, lnb, w2)


def _conv_ffn_kernel(x_ref, g_ref, wup_ref, cw_ref, cb_ref, wdown_ref, o_ref, act_ref, carry_ref,
                     *, tm, chunk):
    @pl.when(pl.program_id(1) == 0)
    def _():
        carry_ref[...] = jnp.zeros_like(carry_ref)

    x = x_ref[0]
    hb = _rms_norm(x, g_ref[...]).astype(BF16)

    def conv_cols(col):
        cols = slice(col, col + chunk)
        up = _dot(hb, wup_ref[:, cols])
        ext = jnp.concatenate([carry_ref[:, cols], up], axis=0)
        carry_ref[:, cols] = up[tm - SUBLANES:, :]
        y = cb_ref[:, cols] + cw_ref[2:3, cols] * up
        y = y + cw_ref[1:2, cols] * pltpu.roll(ext, 1, axis=0)[SUBLANES:]
        y = y + cw_ref[0:1, cols] * pltpu.roll(ext, 2, axis=0)[SUBLANES:]
        return y

    for j in range(D_FF // chunk):
        gate = conv_cols(j * chunk)
        val = conv_cols(D_FF + j * chunk)
        act_ref[:, j * chunk:(j + 1) * chunk] = (gate * jax.nn.sigmoid(gate) * val).astype(BF16)

    o_ref[0] = x + _dot(act_ref[...], wdown_ref[...])


def _conv_ffn(x, g, wup, cw, cb, wdown, *, tm, chunk):
    b, s, d = x.shape
    kern = functools.partial(_conv_ffn_kernel, tm=tm, chunk=chunk)
    return pl.pallas_call(
        kern,
        grid=(b, s // tm),
        in_specs=[
            pl.BlockSpec((1, tm, d), lambda bi, i: (bi, i, 0)),
            _const_spec(g.shape), _const_spec(wup.shape), _const_spec(cw.shape),
            _const_spec(cb.shape), _const_spec(wdown.shape),
        ],
        out_specs=pl.BlockSpec((1, tm, d), lambda bi, i: (bi, i, 0)),
        out_shape=jax.ShapeDtypeStruct((b, s, d), F32),
        scratch_shapes=[pltpu.VMEM((tm, D_FF), BF16), pltpu.VMEM((SUBLANES, 2 * D_FF), F32)],
        compiler_params=pltpu.CompilerParams(
            dimension_semantics=("arbitrary", "arbitrary"), vmem_limit_bytes=VMEM_LIMIT),
        name="conv_ffn",
    )(x, g, wup, cw, cb, wdown)


def _pad_rows(a, rows):
    return jnp.pad(a, ((0, rows - a.shape[0]), (0, 0)))


def kernel(x, even_norm_g, even_w_in, even_b_f, even_q_norm_g, even_k_norm_g, even_w_pool,
           even_pool_scale, even_w_out, odd_norm_g, odd_w_pw1, odd_dw_w, odd_dw_b, odd_ln_g,
           odd_ln_b, odd_w_pw2, ffn_norm_g, ffn_w_up, ffn_conv_w, ffn_conv_b, ffn_w_down):
    depth = ffn_norm_g.shape[0]
    seq = x.shape[1]
    tm = min(512, seq)
    blk = min(2048, seq)
    tril = jnp.tril(jnp.ones((tm, tm), BF16))
    place = _gate_placement()
    row = lambda v: v.reshape(1, -1).astype(F32)

    for layer in range(depth):
        i = layer // 2
        if layer % 2 == 0:
            w_in = even_w_in[i]
            o = 2 * ATTN_DIM
            wqk = w_in[:, :o].astype(BF16)
            wvt = w_in[:, o:o + ATTN_DIM].T.astype(BF16)
            o += ATTN_DIM
            wf = jnp.pad(jnp.tile(w_in[:, o:o + N_HEADS], (1, 3)),
                         ((0, 0), (0, LANES - 3 * N_HEADS))).astype(BF16)
            bf = jnp.pad(jnp.tile(even_b_f[i], 3), (0, LANES - 3 * N_HEADS)).reshape(1, LANES).astype(F32)
            o += N_HEADS
            wu = w_in[:, o:o + POOL_DIM].astype(BF16)
            qg = row(jnp.tile(even_q_norm_g[i], 2))
            kg = row(jnp.tile(even_k_norm_g[i], 2))
            q, k, vt, u = _even_in(x, row(even_norm_g[i]), wqk, wvt, wu, wf, bf, qg, kg, tril, place,
                                   tm=tm)
            a = _fox_attention(q, k, vt, bq=blk)
            x = _even_out(a, u, x, even_w_pool[i].astype(BF16), row(even_pool_scale[i]),
                          even_w_out[i].astype(BF16), tm=tm)
        else:
            x = _conformer(x, row(odd_norm_g[i]), odd_w_pw1[i].astype(BF16),
                           _pad_rows(odd_dw_w[i].astype(F32), CONV_HALO), row(odd_dw_b[i]),
                           row(odd_ln_g[i]), row(odd_ln_b[i]), odd_w_pw2[i].astype(BF16), tm=tm)
        x = _conv_ffn(x, row(ffn_norm_g[layer]), ffn_w_up[layer].astype(BF16),
                      _pad_rows(ffn_conv_w[layer].astype(F32), SUBLANES), row(ffn_conv_b[layer]),
                      ffn_w_down[layer].astype(BF16), tm=tm, chunk=MXU_WIDTH)
    return x
```

```python
import functools
import math

import jax
import jax.numpy as jnp
import numpy as np
from jax import lax
from jax.experimental import pallas as pl
from jax.experimental.pallas import tpu as pltpu

D_MODEL = 1024
N_HEADS = 8
HEAD_DIM = 64
ATTN_DIM = N_HEADS * HEAD_DIM
POOL_WINDOWS = (2, 4, 8, 16)
POOL_GROUP_DIM = 128
POOL_DIM = len(POOL_WINDOWS) * POOL_GROUP_DIM
CONV_WIDTH = 31
D_FF = 2816
FFN_CONV_WIDTH = 3
EPS = 1e-6

LANES = 128
SUBLANES = 8
MXU_WIDTH = 256
POOL_HALO = 16
CONV_HALO = 32
NEG_BIG = -1e30
LOG2E = math.log2(math.e)
BF16_SUBLANES = 16
VT_ROWS = HEAD_DIM + BF16_SUBLANES
KEY_SPLIT = 8
VMEM_LIMIT = 56 * 1024 * 1024

F32 = jnp.float32
BF16 = jnp.bfloat16


def _rms_norm(x, g):
    ms = jnp.mean(x * x, axis=-1, keepdims=True)
    return x * lax.rsqrt(ms + EPS) * g


def _dot(a, b):
    return jnp.dot(a, b, preferred_element_type=F32)


def _dot_nt(a, b):
    return lax.dot_general(a, b, (((1,), (1,)), ((), ())), preferred_element_type=F32)


def _split3(c):
    hi = c.astype(BF16).astype(F32)
    r = c - hi
    mid = r.astype(BF16).astype(F32)
    lo = (r - mid).astype(BF16).astype(F32)
    return hi, mid, lo


def _const_spec(shape):
    zeros = (0,) * len(shape)
    return pl.BlockSpec(shape, lambda *_: zeros, pipeline_mode=pl.Buffered(1))


def _even_in_kernel(x_ref, g_ref, wqk_ref, wvt_ref, wu_ref, wf_ref, bf_ref, qg_ref, kg_ref,
                    tril_ref, place_ref, q_ref, k_ref, vt_ref, u_ref, carry_ref, *, tm):
    @pl.when(pl.program_id(1) == 0)
    def _():
        carry_ref[...] = jnp.zeros_like(carry_ref)

    hb = _rms_norm(x_ref[0], g_ref[...]).astype(BF16)
    qk = _dot(hb, wqk_ref[...])
    vt = _dot_nt(wvt_ref[...], hb).astype(BF16)
    for head in range(N_HEADS):
        vt_ref[0, head * VT_ROWS:head * VT_ROWS + HEAD_DIM] = vt[head * HEAD_DIM:(head + 1) * HEAD_DIM]
        vt_ref[0, head * VT_ROWS + HEAD_DIM:(head + 1) * VT_ROWS] = jnp.ones(
            (VT_ROWS - HEAD_DIM, tm), BF16)
    u_ref[0] = _dot(hb, wu_ref[...])

    fl = _dot(hb, wf_ref[...]) + bf_ref[...]
    logf = jnp.minimum(fl, 0.0) - jnp.log1p(jnp.exp(-jnp.abs(fl)))
    tril = tril_ref[...]
    c = carry_ref[0:1, :]
    for term in _split3(logf):
        c = c + _dot(tril, term.astype(BF16))
    carry_ref[0:1, :] = c[tm - 1:tm, :]
    c = c * LOG2E

    lane = lax.broadcasted_iota(jnp.int32, (tm, LANES), 1)
    hi, mid, lo = _split3(c)
    c3 = jnp.where(lane < N_HEADS, hi, jnp.where(lane < 2 * N_HEADS, mid,
         jnp.where(lane < 3 * N_HEADS, lo, jnp.where(lane == 3 * N_HEADS, 1.0, 0.0))))
    aug_qk = _dot(c3.astype(BF16), place_ref[...])

    low = lane < HEAD_DIM
    inv_dh = 1.0 / HEAD_DIM
    scale = LOG2E / math.sqrt(HEAD_DIM)
    for pair in range(N_HEADS // 2):
        for which, (gain_ref, out_ref) in enumerate(((qg_ref, q_ref), (kg_ref, k_ref))):
            col = which * ATTN_DIM + pair * LANES
            slab = qk[:, col:col + LANES]
            sq = slab * slab
            ss_lo = jnp.sum(jnp.where(low, sq, 0.0), axis=-1, keepdims=True)
            ss_hi = jnp.sum(jnp.where(low, 0.0, sq), axis=-1, keepdims=True)
            inv = jnp.where(low, lax.rsqrt(ss_lo * inv_dh + EPS), lax.rsqrt(ss_hi * inv_dh + EPS))
            normed = slab * inv * gain_ref[...]
            if which == 0:
                normed = normed * scale
            aug = aug_qk[:, col:col + LANES]
            out_ref[0, 2 * pair] = jnp.where(low, normed, aug).astype(BF16)
            out_ref[0, 2 * pair + 1] = jnp.where(low, aug, normed).astype(BF16)


def _gate_placement():
    place = np.zeros((LANES, 2 * ATTN_DIM), np.float32)
    for head in range(N_HEADS):
        base = (head // 2) * LANES + (HEAD_DIM if head % 2 == 0 else 0)
        for term in range(3):
            place[term * N_HEADS + head, base + term] = 1.0
            place[3 * N_HEADS, base + 3 + term] = 1.0
            place[3 * N_HEADS, ATTN_DIM + base + term] = 1.0
            place[term * N_HEADS + head, ATTN_DIM + base + 3 + term] = -1.0
    return jnp.asarray(place, BF16)


def _even_in(x, g, wqk, wvt, wu, wf, bf, qg, kg, tril, place, *, tm):
    b, s, d = x.shape
    kern = functools.partial(_even_in_kernel, tm=tm)
    return pl.pallas_call(
        kern,
        grid=(b, s // tm),
        in_specs=[
            pl.BlockSpec((1, tm, d), lambda bi, i: (bi, i, 0)),
            _const_spec(g.shape), _const_spec(wqk.shape), _const_spec(wvt.shape),
            _const_spec(wu.shape), _const_spec(wf.shape), _const_spec(bf.shape),
            _const_spec(qg.shape), _const_spec(kg.shape), _const_spec(tril.shape),
            _const_spec(place.shape),
        ],
        out_specs=[
            pl.BlockSpec((1, N_HEADS, tm, LANES), lambda bi, i: (bi, 0, i, 0)),
            pl.BlockSpec((1, N_HEADS, tm, LANES), lambda bi, i: (bi, 0, i, 0)),
            pl.BlockSpec((1, N_HEADS * VT_ROWS, tm), lambda bi, i: (bi, 0, i)),
            pl.BlockSpec((1, tm, POOL_DIM), lambda bi, i: (bi, i, 0)),
        ],
        out_shape=[
            jax.ShapeDtypeStruct((b, N_HEADS, s, LANES), BF16),
            jax.ShapeDtypeStruct((b, N_HEADS, s, LANES), BF16),
            jax.ShapeDtypeStruct((b, N_HEADS * VT_ROWS, s), BF16),
            jax.ShapeDtypeStruct((b, s, POOL_DIM), F32),
        ],
        scratch_shapes=[pltpu.VMEM((SUBLANES, LANES), F32)],
        compiler_params=pltpu.CompilerParams(
            dimension_semantics=("arbitrary", "arbitrary"), vmem_limit_bytes=VMEM_LIMIT),
        name="even_in",
    )(x, g, wqk, wvt, wu, wf, bf, qg, kg, tril, place)


def _attn_kernel(q_ref, k_ref, vt_ref, o_ref, sa_ref, sb_ref, m_ref, mb_ref, acc_ref, *, bq):
    bk = bq // KEY_SPLIT
    i = pl.program_id(2)
    chunks = [slice(c, c + MXU_WIDTH) for c in range(0, bq, MXU_WIDTH)]

    def diagonal_tile(d, cols):
        visible = min(bk, cols.stop - d * bk)
        shift = d * bk - cols.start
        return max(visible, 0), (shift if shift + visible - 1 > 0 else None)

    outs = []
    for hh in range(2):
        def fill(s_ref, mb_row, blk_idx, cols, rows=bk, causal_shift=None, hh=hh):
            start = pl.multiple_of(blk_idx * bk, bk)
            s = _dot_nt(k_ref[0, hh, pl.ds(start, rows), :], q_ref[0, hh, cols, :])
            if causal_shift is not None:
                key = lax.broadcasted_iota(jnp.int32, s.shape, 0) + causal_shift
                s = jnp.where(key <= lax.broadcasted_iota(jnp.int32, s.shape, 1), s, NEG_BIG)
            s_ref[0:rows, cols] = s
            mb_ref[mb_row:mb_row + 1, cols] = jnp.max(s, axis=0, keepdims=True)

        def drain(s_ref, mb_row, blk_idx, cols, rows=bk, hh=hh):
            m = m_ref[:, cols]
            m_new = jnp.maximum(m, mb_ref[mb_row:mb_row + 1, cols])
            m_ref[:, cols] = m_new
            p = jnp.exp2(s_ref[0:rows, cols] - m_new).astype(BF16)
            start = pl.multiple_of(blk_idx * bk, bk)
            vb = vt_ref[0, hh * VT_ROWS:(hh + 1) * VT_ROWS, pl.ds(start, rows)]
            acc_ref[:, cols] = jnp.exp2(m - m_new) * acc_ref[:, cols] + _dot(vb, p)

        def fill_diagonal(d, s_ref, mb_row, cols):
            rows, shift = diagonal_tile(d, cols)
            if rows:
                fill(s_ref, mb_row, KEY_SPLIT * i + d, cols, rows=rows, causal_shift=shift)

        def drain_diagonal(d, s_ref, mb_row, cols):
            rows, _ = diagonal_tile(d, cols)
            if rows:
                drain(s_ref, mb_row, KEY_SPLIT * i + d, cols, rows=rows)

        def step(jj, last=False):
            for cols in chunks:
                fill(sb_ref, 1, 2 * jj + 1, cols)
                drain(sa_ref, 0, 2 * jj, cols)
            for cols in chunks:
                if last:
                    fill_diagonal(0, sa_ref, 0, cols)
                else:
                    fill(sa_ref, 0, 2 * jj + 2, cols)
                drain(sb_ref, 1, 2 * jj + 1, cols)

        def body(jj, carry):
            step(2 * jj)
            step(2 * jj + 1)
            return carry

        m_ref[...] = jnp.full((1, bq), NEG_BIG, F32)
        acc_ref[...] = jnp.zeros((VT_ROWS, bq), F32)

        @pl.when(i == 0)
        def _():
            for cols in chunks:
                fill_diagonal(0, sa_ref, 0, cols)

        @pl.when(i > 0)
        def _():
            for cols in chunks:
                fill(sa_ref, 0, 0, cols)

        steps = (KEY_SPLIT // 2) * i
        plain = jnp.maximum(steps - 1, 0)
        lax.fori_loop(0, plain // 2, body, 0)

        @pl.when(plain % 2 == 1)
        def _():
            step(plain - 1)

        @pl.when(i > 0)
        def _():
            step(steps - 1, last=True)

        bufs = ((sa_ref, 0), (sb_ref, 1))
        for d in range(1, KEY_SPLIT):
            for cols in chunks:
                fill_diagonal(d, *bufs[d % 2], cols)
                drain_diagonal(d - 1, *bufs[(d - 1) % 2], cols)
        for cols in chunks:
            drain_diagonal(KEY_SPLIT - 1, *bufs[(KEY_SPLIT - 1) % 2], cols)
        outs.append(acc_ref[:HEAD_DIM] / acc_ref[HEAD_DIM:HEAD_DIM + 1])
    o_ref[0] = jnp.concatenate(outs, axis=0).T.astype(BF16)


def _fox_attention(q, k, vt, *, bq):
    b, nh, s, _ = q.shape
    kern = functools.partial(_attn_kernel, bq=bq)
    return pl.pallas_call(
        kern,
        grid=(b, nh // 2, s // bq),
        in_specs=[
            pl.BlockSpec((1, 2, bq, LANES), lambda bi, p, i: (bi, p, i, 0)),
            pl.BlockSpec((1, 2, s, LANES), lambda bi, p, i: (bi, p, 0, 0)),
            pl.BlockSpec((1, 2 * VT_ROWS, s), lambda bi, p, i: (bi, p, 0)),
        ],
        out_specs=pl.BlockSpec((1, bq, 2 * HEAD_DIM), lambda bi, p, i: (bi, i, p)),
        out_shape=jax.ShapeDtypeStruct((b, s, ATTN_DIM), BF16),
        scratch_shapes=[
            pltpu.VMEM((bq // KEY_SPLIT, bq), F32), pltpu.VMEM((bq // KEY_SPLIT, bq), F32),
            pltpu.VMEM((1, bq), F32), pltpu.VMEM((2, bq), F32),
            pltpu.VMEM((VT_ROWS, bq), F32)],
        compiler_params=pltpu.CompilerParams(
            dimension_semantics=("arbitrary", "arbitrary", "arbitrary"),
            vmem_limit_bytes=VMEM_LIMIT),
        name="fox_attention",
    )(q, k, vt)


def _even_out_kernel(a_ref, u_ref, x_ref, wpool_ref, pscale_ref, wout_ref, o_ref, carry_ref, *, tm):
    i = pl.program_id(1)

    @pl.when(i == 0)
    def _():
        carry_ref[...] = jnp.zeros_like(carry_ref)

    u = u_ref[0]
    pos = (i * tm + 1 + lax.broadcasted_iota(jnp.int32, (tm, 1), 0)).astype(F32)
    parts = [a_ref[0]]
    for g, w in enumerate(POOL_WINDOWS):
        cols = slice(g * POOL_GROUP_DIM, (g + 1) * POOL_GROUP_DIM)
        ug = u[:, cols]
        run = jnp.concatenate([carry_ref[:, cols], ug], axis=0)
        sh = 1
        while sh < w:
            run = run + pltpu.roll(run, sh, axis=0)
            sh *= 2
        pooled = run[POOL_HALO:] / jnp.minimum(pos, float(w))
        mixed = (pooled - ug).astype(BF16)
        parts.append((_dot(mixed, wpool_ref[g]) * pscale_ref[:, cols]).astype(BF16))
    carry_ref[...] = u[tm - POOL_HALO:, :]
    cat = jnp.concatenate(parts, axis=1)
    o_ref[0] = x_ref[0] + _dot(cat, wout_ref[...])


def _even_out(a, u, x, wpool, pscale, wout, *, tm):
    b, s, d = x.shape
    kern = functools.partial(_even_out_kernel, tm=tm)
    return pl.pallas_call(
        kern,
        grid=(b, s // tm),
        in_specs=[
            pl.BlockSpec((1, tm, ATTN_DIM), lambda bi, i: (bi, i, 0)),
            pl.BlockSpec((1, tm, POOL_DIM), lambda bi, i: (bi, i, 0)),
            pl.BlockSpec((1, tm, d), lambda bi, i: (bi, i, 0)),
            _const_spec(wpool.shape), _const_spec(pscale.shape), _const_spec(wout.shape),
        ],
        out_specs=pl.BlockSpec((1, tm, d), lambda bi, i: (bi, i, 0)),
        out_shape=jax.ShapeDtypeStruct((b, s, d), F32),
        scratch_shapes=[pltpu.VMEM((POOL_HALO, POOL_DIM), F32)],
        compiler_params=pltpu.CompilerParams(
            dimension_semantics=("arbitrary", "arbitrary"), vmem_limit_bytes=VMEM_LIMIT),
        name="even_out",
    )(a, u, x, wpool, pscale, wout)


def _conformer_kernel(x_ref, g_ref, w1_ref, dww_ref, dwb_ref, lng_ref, lnb_ref, w2_ref, o_ref,
                      ext_ref, y_ref, *, tm):
    i = pl.program_id(1)

    @pl.when(i == 0)
    def _():
        ext_ref[0:CONV_HALO, :] = jnp.zeros((CONV_HALO, D_MODEL), F32)

    @pl.when(i > 0)
    def _():
        ext_ref[0:CONV_HALO, :] = ext_ref[tm:tm + CONV_HALO, :]

    x = x_ref[0]
    hb = _rms_norm(x, g_ref[...]).astype(BF16)

    def conv_chunk(col):
        cols = slice(col, col + LANES)
        e = ext_ref[:, cols]
        acc = jnp.broadcast_to(dwb_ref[:, cols], (tm, LANES))
        for r in range(SUBLANES):
            er = e if r == 0 else pltpu.roll(e, r, axis=0)
            for a in range(CONV_HALO // SUBLANES):
                shift = SUBLANES * a + r
                if shift >= CONV_WIDTH:
                    continue
                tap = CONV_WIDTH - 1 - shift
                lo = CONV_HALO - SUBLANES * a
                acc = acc + dww_ref[tap:tap + 1, cols] * er[lo:lo + tm]
        y_ref[:, cols] = acc

    for col in range(0, D_MODEL, MXU_WIDTH):
        a = _dot(hb, w1_ref[:, col:col + MXU_WIDTH])
        gate = _dot(hb, w1_ref[:, D_MODEL + col:D_MODEL + col + MXU_WIDTH])
        ext_ref[CONV_HALO:, col:col + MXU_WIDTH] = a * jax.nn.sigmoid(gate)
        for sub in range(0, MXU_WIDTH, LANES):
            conv_chunk(col + sub)

    y = y_ref[...]
    mu = jnp.mean(y, axis=-1, keepdims=True)
    yc = y - mu
    var = jnp.mean(yc * yc, axis=-1, keepdims=True)
    z = yc * lax.rsqrt(var + EPS) * lng_ref[...] + lnb_ref[...]
    z = z * jax.nn.sigmoid(z)
    o_ref[0] = x + _dot(z.astype(BF16), w2_ref[...])


def _conformer(x, g, w1, dww, dwb, lng, lnb, w2, *, tm):
    b, s, d = x.shape
    kern = functools.partial(_conformer_kernel, tm=tm)
    return pl.pallas_call(
        kern,
        grid=(b, s // tm),
        in_specs=[
            pl.BlockSpec((1, tm, d), lambda bi, i: (bi, i, 0)),
            _const_spec(g.shape), _const_spec(w1.shape), _const_spec(dww.shape),
            _const_spec(dwb.shape), _const_spec(lng.shape), _const_spec(lnb.shape),
            _const_spec(w2.shape),
        ],
        out_specs=pl.BlockSpec((1, tm, d), lambda bi, i: (bi, i, 0)),
        out_shape=jax.ShapeDtypeStruct((b, s, d), F32),
        scratch_shapes=[pltpu.VMEM((CONV_HALO + tm, D_MODEL), F32), pltpu.VMEM((tm, D_MODEL), F32)],
        compiler_params=pltpu.CompilerParams(
            dimension_semantics=("arbitrary", "arbitrary"), vmem_limit_bytes=VMEM_LIMIT),
        name="conformer",
    )(x, g, w1, dww, dwb, lng, lnb, w2)


def _conv_ffn_kernel(x_ref, g_ref, wup_ref, cw_ref, cb_ref, wdown_ref, o_ref, act_ref, carry_ref,
                     *, tm, chunk):
    @pl.when(pl.program_id(1) == 0)
    def _():
        carry_ref[...] = jnp.zeros_like(carry_ref)

    x = x_ref[0]
    hb = _rms_norm(x, g_ref[...]).astype(BF16)

    def conv_cols(col):
        cols = slice(col, col + chunk)
        up = _dot(hb, wup_ref[:, cols])
        ext = jnp.concatenate([carry_ref[:, cols], up], axis=0)
        carry_ref[:, cols] = up[tm - SUBLANES:, :]
        y = cb_ref[:, cols] + cw_ref[2:3, cols] * up
        y = y + cw_ref[1:2, cols] * pltpu.roll(ext, 1, axis=0)[SUBLANES:]
        y = y + cw_ref[0:1, cols] * pltpu.roll(ext, 2, axis=0)[SUBLANES:]
        return y

    for j in range(D_FF // chunk):
        gate = conv_cols(j * chunk)
        val = conv_cols(D_FF + j * chunk)
        act_ref[:, j * chunk:(j + 1) * chunk] = (gate * jax.nn.sigmoid(gate) * val).astype(BF16)

    o_ref[0] = x + _dot(act_ref[...], wdown_ref[...])


def _conv_ffn(x, g, wup, cw, cb, wdown, *, tm, chunk):
    b, s, d = x.shape
    kern = functools.partial(_conv_ffn_kernel, tm=tm, chunk=chunk)
    return pl.pallas_call(
        kern,
        grid=(b, s // tm),
        in_specs=[
            pl.BlockSpec((1, tm, d), lambda bi, i: (bi, i, 0)),
            _const_spec(g.shape), _const_spec(wup.shape), _const_spec(cw.shape),
            _const_spec(cb.shape), _const_spec(wdown.shape),
        ],
        out_specs=pl.BlockSpec((1, tm, d), lambda bi, i: (bi, i, 0)),
        out_shape=jax.ShapeDtypeStruct((b, s, d), F32),
        scratch_shapes=[pltpu.VMEM((tm, D_FF), BF16), pltpu.VMEM((SUBLANES, 2 * D_FF), F32)],
        compiler_params=pltpu.CompilerParams(
            dimension_semantics=("arbitrary", "arbitrary"), vmem_limit_bytes=VMEM_LIMIT),
        name="conv_ffn",
    )(x, g, wup, cw, cb, wdown)


def _pad_rows(a, rows):
    return jnp.pad(a, ((0, rows - a.shape[0]), (0, 0)))


def kernel(x, even_norm_g, even_w_in, even_b_f, even_q_norm_g, even_k_norm_g, even_w_pool,
           even_pool_scale, even_w_out, odd_norm_g, odd_w_pw1, odd_dw_w, odd_dw_b, odd_ln_g,
           odd_ln_b, odd_w_pw2, ffn_norm_g, ffn_w_up, ffn_conv_w, ffn_conv_b, ffn_w_down):
    depth = ffn_norm_g.shape[0]
    seq = x.shape[1]
    tm = min(512, seq)
    blk = min(4096, seq)
    tril = jnp.tril(jnp.ones((tm, tm), BF16))
    place = _gate_placement()
    row = lambda v: v.reshape(1, -1).astype(F32)

    for layer in range(depth):
        i = layer // 2
        if layer % 2 == 0:
            w_in = even_w_in[i]
            o = 2 * ATTN_DIM
            wqk = w_in[:, :o].astype(BF16)
            wvt = w_in[:, o:o + ATTN_DIM].T.astype(BF16)
            o += ATTN_DIM
            wf = jnp.pad(jnp.tile(w_in[:, o:o + N_HEADS], (1, 3)),
                         ((0, 0), (0, LANES - 3 * N_HEADS))).astype(BF16)
            bf = jnp.pad(jnp.tile(even_b_f[i], 3), (0, LANES - 3 * N_HEADS)).reshape(1, LANES).astype(F32)
            o += N_HEADS
            wu = w_in[:, o:o + POOL_DIM].astype(BF16)
            qg = row(jnp.tile(even_q_norm_g[i], 2))
            kg = row(jnp.tile(even_k_norm_g[i], 2))
            q, k, vt, u = _even_in(x, row(even_norm_g[i]), wqk, wvt, wu, wf, bf, qg, kg, tril, place,
                                   tm=tm)
            a = _fox_attention(q, k, vt, bq=blk)
            x = _even_out(a, u, x, even_w_pool[i].astype(BF16), row(even_pool_scale[i]),
                          even_w_out[i].astype(BF16), tm=tm)
        else:
            x = _conformer(x, row(odd_norm_g[i]), odd_w_pw1[i].astype(BF16),
                           _pad_rows(odd_dw_w[i].astype(F32), CONV_HALO), row(odd_dw_b[i]),
                           row(odd_ln_g[i]), row(odd_ln_b[i]), odd_w_pw2[i].astype(BF16), tm=tm)
        x = _conv_ffn(x, row(ffn_norm_g[layer]), ffn_w_up[layer].astype(BF16),
                      _pad_rows(ffn_conv_w[layer].astype(F32), SUBLANES), row(ffn_conv_b[layer]),
                      ffn_w_down[layer].astype(BF16), tm=tm, chunk=MXU_WIDTH)
    return x
```

```python
import functools
import math

import jax
import jax.numpy as jnp
import numpy as np
from jax import lax
from jax.experimental import pallas as pl
from jax.experimental.pallas import tpu as pltpu

D_MODEL = 1024
N_HEADS = 8
HEAD_DIM = 64
ATTN_DIM = N_HEADS * HEAD_DIM
POOL_WINDOWS = (2, 4, 8, 16)
POOL_GROUP_DIM = 128
POOL_DIM = len(POOL_WINDOWS) * POOL_GROUP_DIM
CONV_WIDTH = 31
D_FF = 2816
FFN_CONV_WIDTH = 3
EPS = 1e-6

LANES = 128
SUBLANES = 8
MXU_WIDTH = 256
POOL_HALO = 16
CONV_HALO = 32
NEG_BIG = -1e30
LOG2E = math.log2(math.e)
BF16_SUBLANES = 16
VT_ROWS = HEAD_DIM + BF16_SUBLANES
KEY_SPLIT = 4
VMEM_LIMIT = 56 * 1024 * 1024

F32 = jnp.float32
BF16 = jnp.bfloat16


def _rms_norm(x, g):
    ms = jnp.mean(x * x, axis=-1, keepdims=True)
    return x * lax.rsqrt(ms + EPS) * g


def _dot(a, b):
    return jnp.dot(a, b, preferred_element_type=F32)


def _dot_nt(a, b):
    return lax.dot_general(a, b, (((1,), (1,)), ((), ())), preferred_element_type=F32)


def _split3(c):
    hi = c.astype(BF16).astype(F32)
    r = c - hi
    mid = r.astype(BF16).astype(F32)
    lo = (r - mid).astype(BF16).astype(F32)
    return hi, mid, lo


def _const_spec(shape):
    zeros = (0,) * len(shape)
    return pl.BlockSpec(shape, lambda *_: zeros, pipeline_mode=pl.Buffered(1))


def _even_in_kernel(x_ref, g_ref, wqk_ref, wvt_ref, wu_ref, wf_ref, bf_ref, qg_ref, kg_ref,
                    tril_ref, place_ref, q_ref, k_ref, vt_ref, u_ref, carry_ref, *, tm):
    @pl.when(pl.program_id(1) == 0)
    def _():
        carry_ref[...] = jnp.zeros_like(carry_ref)

    hb = _rms_norm(x_ref[0], g_ref[...]).astype(BF16)
    qk = _dot(hb, wqk_ref[...])
    vt = _dot_nt(wvt_ref[...], hb).astype(BF16)
    for head in range(N_HEADS):
        vt_ref[0, head * VT_ROWS:head * VT_ROWS + HEAD_DIM] = vt[head * HEAD_DIM:(head + 1) * HEAD_DIM]
        vt_ref[0, head * VT_ROWS + HEAD_DIM:(head + 1) * VT_ROWS] = jnp.ones(
            (VT_ROWS - HEAD_DIM, tm), BF16)
    u_ref[0] = _dot(hb, wu_ref[...])

    fl = _dot(hb, wf_ref[...]) + bf_ref[...]
    logf = jnp.minimum(fl, 0.0) - jnp.log1p(jnp.exp(-jnp.abs(fl)))
    tril = tril_ref[...]
    c = carry_ref[0:1, :]
    for term in _split3(logf):
        c = c + _dot(tril, term.astype(BF16))
    carry_ref[0:1, :] = c[tm - 1:tm, :]
    c = c * LOG2E

    lane = lax.broadcasted_iota(jnp.int32, (tm, LANES), 1)
    hi, mid, lo = _split3(c)
    c3 = jnp.where(lane < N_HEADS, hi, jnp.where(lane < 2 * N_HEADS, mid,
         jnp.where(lane < 3 * N_HEADS, lo, jnp.where(lane == 3 * N_HEADS, 1.0, 0.0))))
    aug_qk = _dot(c3.astype(BF16), place_ref[...])

    low = lane < HEAD_DIM
    inv_dh = 1.0 / HEAD_DIM
    scale = LOG2E / math.sqrt(HEAD_DIM)
    for pair in range(N_HEADS // 2):
        for which, (gain_ref, out_ref) in enumerate(((qg_ref, q_ref), (kg_ref, k_ref))):
            col = which * ATTN_DIM + pair * LANES
            slab = qk[:, col:col + LANES]
            sq = slab * slab
            ss_lo = jnp.sum(jnp.where(low, sq, 0.0), axis=-1, keepdims=True)
            ss_hi = jnp.sum(jnp.where(low, 0.0, sq), axis=-1, keepdims=True)
            inv = jnp.where(low, lax.rsqrt(ss_lo * inv_dh + EPS), lax.rsqrt(ss_hi * inv_dh + EPS))
            normed = slab * inv * gain_ref[...]
            if which == 0:
                normed = normed * scale
            aug = aug_qk[:, col:col + LANES]
            out_ref[0, 2 * pair] = jnp.where(low, normed, aug).astype(BF16)
            out_ref[0, 2 * pair + 1] = jnp.where(low, aug, normed).astype(BF16)


def _gate_placement():
    place = np.zeros((LANES, 2 * ATTN_DIM), np.float32)
    for head in range(N_HEADS):
        base = (head // 2) * LANES + (HEAD_DIM if head % 2 == 0 else 0)
        for term in range(3):
            place[term * N_HEADS + head, base + term] = 1.0
            place[3 * N_HEADS, base + 3 + term] = 1.0
            place[3 * N_HEADS, ATTN_DIM + base + term] = 1.0
            place[term * N_HEADS + head, ATTN_DIM + base + 3 + term] = -1.0
    return jnp.asarray(place, BF16)


def _even_in(x, g, wqk, wvt, wu, wf, bf, qg, kg, tril, place, *, tm):
    b, s, d = x.shape
    kern = functools.partial(_even_in_kernel, tm=tm)
    return pl.pallas_call(
        kern,
        grid=(b, s // tm),
        in_specs=[
            pl.BlockSpec((1, tm, d), lambda bi, i: (bi, i, 0)),
            _const_spec(g.shape), _const_spec(wqk.shape), _const_spec(wvt.shape),
            _const_spec(wu.shape), _const_spec(wf.shape), _const_spec(bf.shape),
            _const_spec(qg.shape), _const_spec(kg.shape), _const_spec(tril.shape),
            _const_spec(place.shape),
        ],
        out_specs=[
            pl.BlockSpec((1, N_HEADS, tm, LANES), lambda bi, i: (bi, 0, i, 0)),
            pl.BlockSpec((1, N_HEADS, tm, LANES), lambda bi, i: (bi, 0, i, 0)),
            pl.BlockSpec((1, N_HEADS * VT_ROWS, tm), lambda bi, i: (bi, 0, i)),
            pl.BlockSpec((1, tm, POOL_DIM), lambda bi, i: (bi, i, 0)),
        ],
        out_shape=[
            jax.ShapeDtypeStruct((b, N_HEADS, s, LANES), BF16),
            jax.ShapeDtypeStruct((b, N_HEADS, s, LANES), BF16),
            jax.ShapeDtypeStruct((b, N_HEADS * VT_ROWS, s), BF16),
            jax.ShapeDtypeStruct((b, s, POOL_DIM), F32),
        ],
        scratch_shapes=[pltpu.VMEM((SUBLANES, LANES), F32)],
        compiler_params=pltpu.CompilerParams(
            dimension_semantics=("arbitrary", "arbitrary"), vmem_limit_bytes=VMEM_LIMIT),
        name="even_in",
    )(x, g, wqk, wvt, wu, wf, bf, qg, kg, tril, place)


def _attn_kernel(q_ref, k_ref, vt_ref, o_ref, sa_ref, sb_ref, m_ref, mb_ref, acc_ref, ot_ref, *, bq):
    bk = bq // KEY_SPLIT
    i = pl.program_id(2)
    chunks = [slice(c, c + MXU_WIDTH) for c in range(0, bq, MXU_WIDTH)]

    def diagonal_tile(d, cols):
        visible = min(bk, cols.stop - d * bk)
        shift = d * bk - cols.start
        return max(visible, 0), (shift if shift + visible - 1 > 0 else None)

    def one_head(hh, carry):
        def fill(s_ref, mb_row, blk_idx, cols, rows=bk, causal_shift=None):
            start = pl.multiple_of(blk_idx * bk, bk)
            s = _dot_nt(k_ref[0, hh, pl.ds(start, rows), :], q_ref[0, hh, cols, :])
            if causal_shift is not None:
                key = lax.broadcasted_iota(jnp.int32, s.shape, 0) + causal_shift
                s = jnp.where(key <= lax.broadcasted_iota(jnp.int32, s.shape, 1), s, NEG_BIG)
            s_ref[0:rows, cols] = s
            mb_ref[mb_row:mb_row + 1, cols] = jnp.max(s, axis=0, keepdims=True)

        def drain(s_ref, mb_row, blk_idx, cols, rows=bk):
            m = m_ref[:, cols]
            m_new = jnp.maximum(m, mb_ref[mb_row:mb_row + 1, cols])
            m_ref[:, cols] = m_new
            p = jnp.exp2(s_ref[0:rows, cols] - m_new).astype(BF16)
            start = pl.multiple_of(blk_idx * bk, bk)
            vb = vt_ref[0, pl.ds(pl.multiple_of(hh * VT_ROWS, BF16_SUBLANES), VT_ROWS), pl.ds(start, rows)]
            acc_ref[:, cols] = jnp.exp2(m - m_new) * acc_ref[:, cols] + _dot(vb, p)

        def fill_diagonal(d, s_ref, mb_row, cols):
            rows, shift = diagonal_tile(d, cols)
            if rows:
                fill(s_ref, mb_row, KEY_SPLIT * i + d, cols, rows=rows, causal_shift=shift)

        def drain_diagonal(d, s_ref, mb_row, cols):
            rows, _ = diagonal_tile(d, cols)
            if rows:
                drain(s_ref, mb_row, KEY_SPLIT * i + d, cols, rows=rows)

        def step(jj, last=False):
            for cols in chunks:
                fill(sb_ref, 1, 2 * jj + 1, cols)
                drain(sa_ref, 0, 2 * jj, cols)
            for cols in chunks:
                if last:
                    fill_diagonal(0, sa_ref, 0, cols)
                else:
                    fill(sa_ref, 0, 2 * jj + 2, cols)
                drain(sb_ref, 1, 2 * jj + 1, cols)

        def body(jj, carry):
            step(2 * jj)
            step(2 * jj + 1)
            return carry

        m_ref[...] = jnp.full((1, bq), NEG_BIG, F32)
        acc_ref[...] = jnp.zeros((VT_ROWS, bq), F32)

        @pl.when(i == 0)
        def _():
            for cols in chunks:
                fill_diagonal(0, sa_ref, 0, cols)

        @pl.when(i > 0)
        def _():
            for cols in chunks:
                fill(sa_ref, 0, 0, cols)

        steps = (KEY_SPLIT // 2) * i
        plain = jnp.maximum(steps - 1, 0)
        lax.fori_loop(0, plain // 2, body, 0)

        @pl.when(plain % 2 == 1)
        def _():
            step(plain - 1)

        @pl.when(i > 0)
        def _():
            step(steps - 1, last=True)

        bufs = ((sa_ref, 0), (sb_ref, 1))
        for d in range(1, KEY_SPLIT):
            for cols in chunks:
                fill_diagonal(d, *bufs[d % 2], cols)
                drain_diagonal(d - 1, *bufs[(d - 1) % 2], cols)
        for cols in chunks:
            drain_diagonal(KEY_SPLIT - 1, *bufs[(KEY_SPLIT - 1) % 2], cols)
        ot_ref[pl.ds(pl.multiple_of(hh * HEAD_DIM, HEAD_DIM), HEAD_DIM), :] = (
            acc_ref[:HEAD_DIM] / acc_ref[HEAD_DIM:HEAD_DIM + 1])
        return carry

    lax.fori_loop(0, 2, one_head, 0)
    o_ref[0] = ot_ref[...].T.astype(BF16)


def _fox_attention(q, k, vt, *, bq):
    b, nh, s, _ = q.shape
    kern = functools.partial(_attn_kernel, bq=bq)
    return pl.pallas_call(
        kern,
        grid=(b, nh // 2, s // bq),
        in_specs=[
            pl.BlockSpec((1, 2, bq, LANES), lambda bi, p, i: (bi, p, i, 0)),
            pl.BlockSpec((1, 2, s, LANES), lambda bi, p, i: (bi, p, 0, 0)),
            pl.BlockSpec((1, 2 * VT_ROWS, s), lambda bi, p, i: (bi, p, 0)),
        ],
        out_specs=pl.BlockSpec((1, bq, 2 * HEAD_DIM), lambda bi, p, i: (bi, i, p)),
        out_shape=jax.ShapeDtypeStruct((b, s, ATTN_DIM), BF16),
        scratch_shapes=[
            pltpu.VMEM((bq // KEY_SPLIT, bq), F32), pltpu.VMEM((bq // KEY_SPLIT, bq), F32),
            pltpu.VMEM((1, bq), F32), pltpu.VMEM((2, bq), F32),
            pltpu.VMEM((VT_ROWS, bq), F32),
            pltpu.VMEM((2 * HEAD_DIM, bq), F32)],
        compiler_params=pltpu.CompilerParams(
            dimension_semantics=("arbitrary", "arbitrary", "arbitrary"),
            vmem_limit_bytes=VMEM_LIMIT),
        name="fox_attention",
    )(q, k, vt)


def _even_out_kernel(a_ref, u_ref, x_ref, wpool_ref, pscale_ref, wout_ref, o_ref, carry_ref, *, tm):
    i = pl.program_id(1)

    @pl.when(i == 0)
    def _():
        carry_ref[...] = jnp.zeros_like(carry_ref)

    u = u_ref[0]
    pos = (i * tm + 1 + lax.broadcasted_iota(jnp.int32, (tm, 1), 0)).astype(F32)
    parts = [a_ref[0]]
    for g, w in enumerate(POOL_WINDOWS):
        cols = slice(g * POOL_GROUP_DIM, (g + 1) * POOL_GROUP_DIM)
        ug = u[:, cols]
        run = jnp.concatenate([carry_ref[:, cols], ug], axis=0)
        sh = 1
        while sh < w:
            run = run + pltpu.roll(run, sh, axis=0)
            sh *= 2
        pooled = run[POOL_HALO:] / jnp.minimum(pos, float(w))
        mixed = (pooled - ug).astype(BF16)
        parts.append((_dot(mixed, wpool_ref[g]) * pscale_ref[:, cols]).astype(BF16))
    carry_ref[...] = u[tm - POOL_HALO:, :]
    cat = jnp.concatenate(parts, axis=1)
    o_ref[0] = x_ref[0] + _dot(cat, wout_ref[...])


def _even_out(a, u, x, wpool, pscale, wout, *, tm):
    b, s, d = x.shape
    kern = functools.partial(_even_out_kernel, tm=tm)
    return pl.pallas_call(
        kern,
        grid=(b, s // tm),
        in_specs=[
            pl.BlockSpec((1, tm, ATTN_DIM), lambda bi, i: (bi, i, 0)),
            pl.BlockSpec((1, tm, POOL_DIM), lambda bi, i: (bi, i, 0)),
            pl.BlockSpec((1, tm, d), lambda bi, i: (bi, i, 0)),
            _const_spec(wpool.shape), _const_spec(pscale.shape), _const_spec(wout.shape),
        ],
        out_specs=pl.BlockSpec((1, tm, d), lambda bi, i: (bi, i, 0)),
        out_shape=jax.ShapeDtypeStruct((b, s, d), F32),
        scratch_shapes=[pltpu.VMEM((POOL_HALO, POOL_DIM), F32)],
        compiler_params=pltpu.CompilerParams(
            dimension_semantics=("arbitrary", "arbitrary"), vmem_limit_bytes=VMEM_LIMIT),
        name="even_out",
    )(a, u, x, wpool, pscale, wout)


def _conformer_kernel(x_ref, g_ref, w1_ref, dww_ref, dwb_ref, lng_ref, lnb_ref, w2_ref, o_ref,
                      ext_ref, y_ref, *, tm):
    i = pl.program_id(1)

    @pl.when(i == 0)
    def _():
        ext_ref[0:CONV_HALO, :] = jnp.zeros((CONV_HALO, D_MODEL), F32)

    @pl.when(i > 0)
    def _():
        ext_ref[0:CONV_HALO, :] = ext_ref[tm:tm + CONV_HALO, :]

    x = x_ref[0]
    hb = _rms_norm(x, g_ref[...]).astype(BF16)

    def conv_chunk(col):
        cols = slice(col, col + LANES)
        e = ext_ref[:, cols]
        acc = jnp.broadcast_to(dwb_ref[:, cols], (tm, LANES))
        for r in range(SUBLANES):
            er = e if r == 0 else pltpu.roll(e, r, axis=0)
            for a in range(CONV_HALO // SUBLANES):
                shift = SUBLANES * a + r
                if shift >= CONV_WIDTH:
                    continue
                tap = CONV_WIDTH - 1 - shift
                lo = CONV_HALO - SUBLANES * a
                acc = acc + dww_ref[tap:tap + 1, cols] * er[lo:lo + tm]
        y_ref[:, cols] = acc

    for col in range(0, D_MODEL, MXU_WIDTH):
        a = _dot(hb, w1_ref[:, col:col + MXU_WIDTH])
        gate = _dot(hb, w1_ref[:, D_MODEL + col:D_MODEL + col + MXU_WIDTH])
        ext_ref[CONV_HALO:, col:col + MXU_WIDTH] = a * jax.nn.sigmoid(gate)
        for sub in range(0, MXU_WIDTH, LANES):
            conv_chunk(col + sub)

    y = y_ref[...]
    mu = jnp.mean(y, axis=-1, keepdims=True)
    yc = y - mu
    var = jnp.mean(yc * yc, axis=-1, keepdims=True)
    z = yc * lax.rsqrt(var + EPS) * lng_ref[...] + lnb_ref[...]
    z = z * jax.nn.sigmoid(z)
    o_ref[0] = x + _dot(z.astype(BF16), w2_ref[...])


def _conformer(x, g, w1, dww, dwb, lng, lnb, w2, *, tm):
    b, s, d = x.shape
    kern = functools.partial(_conformer_kernel, tm=tm)
    return pl.pallas_call(
        kern,
        grid=(b, s // tm),
        in_specs=[
            pl.BlockSpec((1, tm, d), lambda bi, i: (bi, i, 0)),
            _const_spec(g.shape), _const_spec(w1.shape), _const_spec(dww.shape),
            _const_spec(dwb.shape), _const_spec(lng.shape), _const_spec(lnb.shape),
            _const_spec(w2.shape),
        ],
        out_specs=pl.BlockSpec((1, tm, d), lambda bi, i: (bi, i, 0)),
        out_shape=jax.ShapeDtypeStruct((b, s, d), F32),
        scratch_shapes=[pltpu.VMEM((CONV_HALO + tm, D_MODEL), F32), pltpu.VMEM((tm, D_MODEL), F32)],
        compiler_params=pltpu.CompilerParams(
            dimension_semantics=("arbitrary", "arbitrary"), vmem_limit_bytes=VMEM_LIMIT),
        name="conformer",
    )(x, g, w1, dww, dwb, lng, lnb, w2)


def _conv_ffn_kernel(x_ref, g_ref, wup_ref, cw_ref, cb_ref, wdown_ref, o_ref, act_ref, carry_ref,
                     *, tm, chunk):
    @pl.when(pl.program_id(1) == 0)
    def _():
        carry_ref[...] = jnp.zeros_like(carry_ref)

    x = x_ref[0]
    hb = _rms_norm(x, g_ref[...]).astype(BF16)

    def conv_cols(col):
        cols = slice(col, col + chunk)
        up = _dot(hb, wup_ref[:, cols])
        ext = jnp.concatenate([carry_ref[:, cols], up], axis=0)
        carry_ref[:, cols] = up[tm - SUBLANES:, :]
        y = cb_ref[:, cols] + cw_ref[2:3, cols] * up
        y = y + cw_ref[1:2, cols] * pltpu.roll(ext, 1, axis=0)[SUBLANES:]
        y = y + cw_ref[0:1, cols] * pltpu.roll(ext, 2, axis=0)[SUBLANES:]
        return y

    for j in range(D_FF // chunk):
        gate = conv_cols(j * chunk)
        val = conv_cols(D_FF + j * chunk)
        act_ref[:, j * chunk:(j + 1) * chunk] = (gate * jax.nn.sigmoid(gate) * val).astype(BF16)

    o_ref[0] = x + _dot(act_ref[...], wdown_ref[...])


def _conv_ffn(x, g, wup, cw, cb, wdown, *, tm, chunk):
    b, s, d = x.shape
    kern = functools.partial(_conv_ffn_kernel, tm=tm, chunk=chunk)
    return pl.pallas_call(
        kern,
        grid=(b, s // tm),
        in_specs=[
            pl.BlockSpec((1, tm, d), lambda bi, i: (bi, i, 0)),
            _const_spec(g.shape), _const_spec(wup.shape), _const_spec(cw.shape),
            _const_spec(cb.shape), _const_spec(wdown.shape),
        ],
        out_specs=pl.BlockSpec((1, tm, d), lambda bi, i: (bi, i, 0)),
        out_shape=jax.ShapeDtypeStruct((b, s, d), F32),
        scratch_shapes=[pltpu.VMEM((tm, D_FF), BF16), pltpu.VMEM((SUBLANES, 2 * D_FF), F32)],
        compiler_params=pltpu.CompilerParams(
            dimension_semantics=("arbitrary", "arbitrary"), vmem_limit_bytes=VMEM_LIMIT),
        name="conv_ffn",
    )(x, g, wup, cw, cb, wdown)


def _pad_rows(a, rows):
    return jnp.pad(a, ((0, rows - a.shape[0]), (0, 0)))


def kernel(x, even_norm_g, even_w_in, even_b_f, even_q_norm_g, even_k_norm_g, even_w_pool,
           even_pool_scale, even_w_out, odd_norm_g, odd_w_pw1, odd_dw_w, odd_dw_b, odd_ln_g,
           odd_ln_b, odd_w_pw2, ffn_norm_g, ffn_w_up, ffn_conv_w, ffn_conv_b, ffn_w_down):
    depth = ffn_norm_g.shape[0]
    seq = x.shape[1]
    tm = min(512, seq)
    blk = min(2048, seq)
    tril = jnp.tril(jnp.ones((tm, tm), BF16))
    place = _gate_placement()
    row = lambda v: v.reshape(1, -1).astype(F32)

    for layer in range(depth):
        i = layer // 2
        if layer % 2 == 0:
            w_in = even_w_in[i]
            o = 2 * ATTN_DIM
            wqk = w_in[:, :o].astype(BF16)
            wvt = w_in[:, o:o + ATTN_DIM].T.astype(BF16)
            o += ATTN_DIM
            wf = jnp.pad(jnp.tile(w_in[:, o:o + N_HEADS], (1, 3)),
                         ((0, 0), (0, LANES - 3 * N_HEADS))).astype(BF16)
            bf = jnp.pad(jnp.tile(even_b_f[i], 3), (0, LANES - 3 * N_HEADS)).reshape(1, LANES).astype(F32)
            o += N_HEADS
            wu = w_in[:, o:o + POOL_DIM].astype(BF16)
            qg = row(jnp.tile(even_q_norm_g[i], 2))
            kg = row(jnp.tile(even_k_norm_g[i], 2))
            q, k, vt, u = _even_in(x, row(even_norm_g[i]), wqk, wvt, wu, wf, bf, qg, kg, tril, place,
                                   tm=tm)
            a = _fox_attention(q, k, vt, bq=blk)
            x = _even_out(a, u, x, even_w_pool[i].astype(BF16), row(even_pool_scale[i]),
                          even_w_out[i].astype(BF16), tm=tm)
        else:
            x = _conformer(x, row(odd_norm_g[i]), odd_w_pw1[i].astype(BF16),
                           _pad_rows(odd_dw_w[i].astype(F32), CONV_HALO), row(odd_dw_b[i]),
                           row(odd_ln_g[i]), row(odd_ln_b[i]), odd_w_pw2[i].astype(BF16), tm=tm)
        x = _conv_ffn(x, row(ffn_norm_g[layer]), ffn_w_up[layer].astype(BF16),
                      _pad_rows(ffn_conv_w[layer].astype(F32), SUBLANES), row(ffn_conv_b[layer]),
                      ffn_w_down[layer].astype(BF16), tm=tm, chunk=MXU_WIDTH)
    return x
```

```python
import functools
import math

import jax
import jax.numpy as jnp
import numpy as np
from jax import lax
from jax.experimental import pallas as pl
from jax.experimental.pallas import tpu as pltpu

D_MODEL = 1024
N_HEADS = 8
HEAD_DIM = 64
ATTN_DIM = N_HEADS * HEAD_DIM
POOL_WINDOWS = (2, 4, 8, 16)
POOL_GROUP_DIM = 128
POOL_DIM = len(POOL_WINDOWS) * POOL_GROUP_DIM
CONV_WIDTH = 31
D_FF = 2816
FFN_CONV_WIDTH = 3
EPS = 1e-6

LANES = 128
SUBLANES = 8
MXU_WIDTH = 256
POOL_HALO = 16
CONV_HALO = 32
NEG_BIG = -1e30
LOG2E = math.log2(math.e)
BF16_SUBLANES = 16
VT_ROWS = HEAD_DIM + BF16_SUBLANES
KEY_SPLIT = 4
VMEM_LIMIT = 56 * 1024 * 1024

F32 = jnp.float32
BF16 = jnp.bfloat16


def _rms_norm(x, g):
    ms = jnp.mean(x * x, axis=-1, keepdims=True)
    return x * lax.rsqrt(ms + EPS) * g


def _dot(a, b):
    return jnp.dot(a, b, preferred_element_type=F32)


def _dot_nt(a, b):
    return lax.dot_general(a, b, (((1,), (1,)), ((), ())), preferred_element_type=F32)


def _split3(c):
    hi = c.astype(BF16).astype(F32)
    r = c - hi
    mid = r.astype(BF16).astype(F32)
    lo = (r - mid).astype(BF16).astype(F32)
    return hi, mid, lo


def _const_spec(shape):
    zeros = (0,) * len(shape)
    return pl.BlockSpec(shape, lambda *_: zeros, pipeline_mode=pl.Buffered(1))


def _even_in_kernel(x_ref, g_ref, wqk_ref, wvt_ref, wu_ref, wf_ref, bf_ref, qg_ref, kg_ref,
                    tril_ref, place_ref, q_ref, k_ref, vt_ref, u_ref, carry_ref, *, tm):
    @pl.when(pl.program_id(1) == 0)
    def _():
        carry_ref[...] = jnp.zeros_like(carry_ref)

    hb = _rms_norm(x_ref[0], g_ref[...]).astype(BF16)
    qk = _dot(hb, wqk_ref[...])
    vt = _dot_nt(wvt_ref[...], hb).astype(BF16)
    for head in range(N_HEADS):
        vt_ref[0, head * VT_ROWS:head * VT_ROWS + HEAD_DIM] = vt[head * HEAD_DIM:(head + 1) * HEAD_DIM]
        vt_ref[0, head * VT_ROWS + HEAD_DIM:(head + 1) * VT_ROWS] = jnp.ones(
            (VT_ROWS - HEAD_DIM, tm), BF16)
    u_ref[0] = _dot(hb, wu_ref[...])

    fl = _dot(hb, wf_ref[...]) + bf_ref[...]
    logf = jnp.minimum(fl, 0.0) - jnp.log1p(jnp.exp(-jnp.abs(fl)))
    tril = tril_ref[...]
    c = carry_ref[0:1, :]
    for term in _split3(logf):
        c = c + _dot(tril, term.astype(BF16))
    carry_ref[0:1, :] = c[tm - 1:tm, :]
    c = c * LOG2E

    lane = lax.broadcasted_iota(jnp.int32, (tm, LANES), 1)
    hi, mid, lo = _split3(c)
    c3 = jnp.where(lane < N_HEADS, hi, jnp.where(lane < 2 * N_HEADS, mid,
         jnp.where(lane < 3 * N_HEADS, lo, jnp.where(lane == 3 * N_HEADS, 1.0, 0.0))))
    aug_qk = _dot(c3.astype(BF16), place_ref[...])

    low = lane < HEAD_DIM
    inv_dh = 1.0 / HEAD_DIM
    scale = LOG2E / math.sqrt(HEAD_DIM)
    for pair in range(N_HEADS // 2):
        for which, (gain_ref, out_ref) in enumerate(((qg_ref, q_ref), (kg_ref, k_ref))):
            col = which * ATTN_DIM + pair * LANES
            slab = qk[:, col:col + LANES]
            sq = slab * slab
            ss_lo = jnp.sum(jnp.where(low, sq, 0.0), axis=-1, keepdims=True)
            ss_hi = jnp.sum(jnp.where(low, 0.0, sq), axis=-1, keepdims=True)
            inv = jnp.where(low, lax.rsqrt(ss_lo * inv_dh + EPS), lax.rsqrt(ss_hi * inv_dh + EPS))
            normed = slab * inv * gain_ref[...]
            if which == 0:
                normed = normed * scale
            aug = aug_qk[:, col:col + LANES]
            out_ref[0, 2 * pair] = jnp.where(low, normed, aug).astype(BF16)
            out_ref[0, 2 * pair + 1] = jnp.where(low, aug, normed).astype(BF16)


def _gate_placement():
    place = np.zeros((LANES, 2 * ATTN_DIM), np.float32)
    for head in range(N_HEADS):
        base = (head // 2) * LANES + (HEAD_DIM if head % 2 == 0 else 0)
        for term in range(3):
            place[term * N_HEADS + head, base + term] = 1.0
            place[3 * N_HEADS, base + 3 + term] = 1.0
            place[3 * N_HEADS, ATTN_DIM + base + term] = 1.0
            place[term * N_HEADS + head, ATTN_DIM + base + 3 + term] = -1.0
    return jnp.asarray(place, BF16)


def _even_in(x, g, wqk, wvt, wu, wf, bf, qg, kg, tril, place, *, tm):
    b, s, d = x.shape
    kern = functools.partial(_even_in_kernel, tm=tm)
    return pl.pallas_call(
        kern,
        grid=(b, s // tm),
        in_specs=[
            pl.BlockSpec((1, tm, d), lambda bi, i: (bi, i, 0)),
            _const_spec(g.shape), _const_spec(wqk.shape), _const_spec(wvt.shape),
            _const_spec(wu.shape), _const_spec(wf.shape), _const_spec(bf.shape),
            _const_spec(qg.shape), _const_spec(kg.shape), _const_spec(tril.shape),
            _const_spec(place.shape),
        ],
        out_specs=[
            pl.BlockSpec((1, N_HEADS, tm, LANES), lambda bi, i: (bi, 0, i, 0)),
            pl.BlockSpec((1, N_HEADS, tm, LANES), lambda bi, i: (bi, 0, i, 0)),
            pl.BlockSpec((1, N_HEADS * VT_ROWS, tm), lambda bi, i: (bi, 0, i)),
            pl.BlockSpec((1, tm, POOL_DIM), lambda bi, i: (bi, i, 0)),
        ],
        out_shape=[
            jax.ShapeDtypeStruct((b, N_HEADS, s, LANES), BF16),
            jax.ShapeDtypeStruct((b, N_HEADS, s, LANES), BF16),
            jax.ShapeDtypeStruct((b, N_HEADS * VT_ROWS, s), BF16),
            jax.ShapeDtypeStruct((b, s, POOL_DIM), F32),
        ],
        scratch_shapes=[pltpu.VMEM((SUBLANES, LANES), F32)],
        compiler_params=pltpu.CompilerParams(
            dimension_semantics=("arbitrary", "arbitrary"), vmem_limit_bytes=VMEM_LIMIT),
        name="even_in",
    )(x, g, wqk, wvt, wu, wf, bf, qg, kg, tril, place)


def _attn_kernel(q_ref, k_ref, vt_ref, o_ref, sa_ref, sb_ref, m_ref, mb_ref, acc_ref, ot_ref, *, bq):
    bk = bq // KEY_SPLIT
    i = pl.program_id(2)
    chunks = [slice(c, c + MXU_WIDTH) for c in range(0, bq, MXU_WIDTH)]

    def diagonal_tile(d, cols):
        visible = min(bk, cols.stop - d * bk)
        shift = d * bk - cols.start
        return max(visible, 0), (shift if shift + visible - 1 > 0 else None)

    def one_head(hh, carry):
        def fill(s_ref, mb_row, blk_idx, cols, rows=bk, causal_shift=None):
            start = pl.multiple_of(blk_idx * bk, bk)
            s = _dot_nt(k_ref[0, hh, pl.ds(start, rows), :], q_ref[0, hh, cols, :])
            if causal_shift is not None:
                key = lax.broadcasted_iota(jnp.int32, s.shape, 0) + causal_shift
                s = jnp.where(key <= lax.broadcasted_iota(jnp.int32, s.shape, 1), s, NEG_BIG)
            s_ref[cols.start // MXU_WIDTH, 0:rows, :] = s
            mb_ref[mb_row:mb_row + 1, cols] = jnp.max(s, axis=0, keepdims=True)

        def drain(s_ref, mb_row, blk_idx, cols, rows=bk):
            m = m_ref[:, cols]
            m_new = jnp.maximum(m, mb_ref[mb_row:mb_row + 1, cols])
            m_ref[:, cols] = m_new
            p = jnp.exp2(s_ref[cols.start // MXU_WIDTH, 0:rows, :] - m_new).astype(BF16)
            start = pl.multiple_of(blk_idx * bk, bk)
            vb = vt_ref[0, pl.ds(pl.multiple_of(hh * VT_ROWS, BF16_SUBLANES), VT_ROWS), pl.ds(start, rows)]
            acc_ref[:, cols] = jnp.exp2(m - m_new) * acc_ref[:, cols] + _dot(vb, p)

        def fill_diagonal(d, s_ref, mb_row, cols):
            rows, shift = diagonal_tile(d, cols)
            if rows:
                fill(s_ref, mb_row, KEY_SPLIT * i + d, cols, rows=rows, causal_shift=shift)

        def drain_diagonal(d, s_ref, mb_row, cols):
            rows, _ = diagonal_tile(d, cols)
            if rows:
                drain(s_ref, mb_row, KEY_SPLIT * i + d, cols, rows=rows)

        def step(jj, last=False):
            for cols in chunks:
                fill(sb_ref, 1, 2 * jj + 1, cols)
                drain(sa_ref, 0, 2 * jj, cols)
            for cols in chunks:
                if last:
                    fill_diagonal(0, sa_ref, 0, cols)
                else:
                    fill(sa_ref, 0, 2 * jj + 2, cols)
                drain(sb_ref, 1, 2 * jj + 1, cols)

        def body(jj, carry):
            step(2 * jj)
            step(2 * jj + 1)
            return carry

        m_ref[...] = jnp.full((1, bq), NEG_BIG, F32)
        acc_ref[...] = jnp.zeros((VT_ROWS, bq), F32)

        @pl.when(i == 0)
        def _():
            for cols in chunks:
                fill_diagonal(0, sa_ref, 0, cols)

        @pl.when(i > 0)
        def _():
            for cols in chunks:
                fill(sa_ref, 0, 0, cols)

        steps = (KEY_SPLIT // 2) * i
        plain = jnp.maximum(steps - 1, 0)
        lax.fori_loop(0, plain // 2, body, 0)

        @pl.when(plain % 2 == 1)
        def _():
            step(plain - 1)

        @pl.when(i > 0)
        def _():
            step(steps - 1, last=True)

        bufs = ((sa_ref, 0), (sb_ref, 1))
        for d in range(1, KEY_SPLIT):
            for cols in chunks:
                fill_diagonal(d, *bufs[d % 2], cols)
                drain_diagonal(d - 1, *bufs[(d - 1) % 2], cols)
        for cols in chunks:
            drain_diagonal(KEY_SPLIT - 1, *bufs[(KEY_SPLIT - 1) % 2], cols)
        ot_ref[pl.ds(pl.multiple_of(hh * HEAD_DIM, HEAD_DIM), HEAD_DIM), :] = (
            acc_ref[:HEAD_DIM] / acc_ref[HEAD_DIM:HEAD_DIM + 1])
        return carry

    lax.fori_loop(0, 2, one_head, 0)
    o_ref[0] = ot_ref[...].T.astype(BF16)


def _fox_attention(q, k, vt, *, bq):
    b, nh, s, _ = q.shape
    kern = functools.partial(_attn_kernel, bq=bq)
    return pl.pallas_call(
        kern,
        grid=(b, nh // 2, s // bq),
        in_specs=[
            pl.BlockSpec((1, 2, bq, LANES), lambda bi, p, i: (bi, p, i, 0)),
            pl.BlockSpec((1, 2, s, LANES), lambda bi, p, i: (bi, p, 0, 0)),
            pl.BlockSpec((1, 2 * VT_ROWS, s), lambda bi, p, i: (bi, p, 0)),
        ],
        out_specs=pl.BlockSpec((1, bq, 2 * HEAD_DIM), lambda bi, p, i: (bi, i, p)),
        out_shape=jax.ShapeDtypeStruct((b, s, ATTN_DIM), BF16),
        scratch_shapes=[
            pltpu.VMEM((bq // MXU_WIDTH, bq // KEY_SPLIT, MXU_WIDTH), F32),
            pltpu.VMEM((bq // MXU_WIDTH, bq // KEY_SPLIT, MXU_WIDTH), F32),
            pltpu.VMEM((1, bq), F32), pltpu.VMEM((2, bq), F32),
            pltpu.VMEM((VT_ROWS, bq), F32),
            pltpu.VMEM((2 * HEAD_DIM, bq), F32)],
        compiler_params=pltpu.CompilerParams(
            dimension_semantics=("arbitrary", "arbitrary", "arbitrary"),
            vmem_limit_bytes=VMEM_LIMIT),
        name="fox_attention",
    )(q, k, vt)


def _even_out_kernel(a_ref, u_ref, x_ref, wpool_ref, pscale_ref, wout_ref, o_ref, carry_ref, *, tm):
    i = pl.program_id(1)

    @pl.when(i == 0)
    def _():
        carry_ref[...] = jnp.zeros_like(carry_ref)

    u = u_ref[0]
    pos = (i * tm + 1 + lax.broadcasted_iota(jnp.int32, (tm, 1), 0)).astype(F32)
    parts = [a_ref[0]]
    for g, w in enumerate(POOL_WINDOWS):
        cols = slice(g * POOL_GROUP_DIM, (g + 1) * POOL_GROUP_DIM)
        ug = u[:, cols]
        run = jnp.concatenate([carry_ref[:, cols], ug], axis=0)
        sh = 1
        while sh < w:
            run = run + pltpu.roll(run, sh, axis=0)
            sh *= 2
        pooled = run[POOL_HALO:] / jnp.minimum(pos, float(w))
        mixed = (pooled - ug).astype(BF16)
        parts.append((_dot(mixed, wpool_ref[g]) * pscale_ref[:, cols]).astype(BF16))
    carry_ref[...] = u[tm - POOL_HALO:, :]
    cat = jnp.concatenate(parts, axis=1)
    o_ref[0] = x_ref[0] + _dot(cat, wout_ref[...])


def _even_out(a, u, x, wpool, pscale, wout, *, tm):
    b, s, d = x.shape
    kern = functools.partial(_even_out_kernel, tm=tm)
    return pl.pallas_call(
        kern,
        grid=(b, s // tm),
        in_specs=[
            pl.BlockSpec((1, tm, ATTN_DIM), lambda bi, i: (bi, i, 0)),
            pl.BlockSpec((1, tm, POOL_DIM), lambda bi, i: (bi, i, 0)),
            pl.BlockSpec((1, tm, d), lambda bi, i: (bi, i, 0)),
            _const_spec(wpool.shape), _const_spec(pscale.shape), _const_spec(wout.shape),
        ],
        out_specs=pl.BlockSpec((1, tm, d), lambda bi, i: (bi, i, 0)),
        out_shape=jax.ShapeDtypeStruct((b, s, d), F32),
        scratch_shapes=[pltpu.VMEM((POOL_HALO, POOL_DIM), F32)],
        compiler_params=pltpu.CompilerParams(
            dimension_semantics=("arbitrary", "arbitrary"), vmem_limit_bytes=VMEM_LIMIT),
        name="even_out",
    )(a, u, x, wpool, pscale, wout)


def _conformer_kernel(x_ref, g_ref, w1_ref, dww_ref, dwb_ref, lng_ref, lnb_ref, w2_ref, o_ref,
                      ext_ref, y_ref, *, tm):
    i = pl.program_id(1)

    @pl.when(i == 0)
    def _():
        ext_ref[0:CONV_HALO, :] = jnp.zeros((CONV_HALO, D_MODEL), F32)

    @pl.when(i > 0)
    def _():
        ext_ref[0:CONV_HALO, :] = ext_ref[tm:tm + CONV_HALO, :]

    x = x_ref[0]
    hb = _rms_norm(x, g_ref[...]).astype(BF16)

    def conv_chunk(col):
        cols = slice(col, col + LANES)
        e = ext_ref[:, cols]
        acc = jnp.broadcast_to(dwb_ref[:, cols], (tm, LANES))
        for r in range(SUBLANES):
            er = e if r == 0 else pltpu.roll(e, r, axis=0)
            for a in range(CONV_HALO // SUBLANES):
                shift = SUBLANES * a + r
                if shift >= CONV_WIDTH:
                    continue
                tap = CONV_WIDTH - 1 - shift
                lo = CONV_HALO - SUBLANES * a
                acc = acc + dww_ref[tap:tap + 1, cols] * er[lo:lo + tm]
        y_ref[:, cols] = acc

    for col in range(0, D_MODEL, MXU_WIDTH):
        a = _dot(hb, w1_ref[:, col:col + MXU_WIDTH])
        gate = _dot(hb, w1_ref[:, D_MODEL + col:D_MODEL + col + MXU_WIDTH])
        ext_ref[CONV_HALO:, col:col + MXU_WIDTH] = a * jax.nn.sigmoid(gate)
        for sub in range(0, MXU_WIDTH, LANES):
            conv_chunk(col + sub)

    y = y_ref[...]
    mu = jnp.mean(y, axis=-1, keepdims=True)
    yc = y - mu
    var = jnp.mean(yc * yc, axis=-1, keepdims=True)
    z = yc * lax.rsqrt(var + EPS) * lng_ref[...] + lnb_ref[...]
    z = z * jax.nn.sigmoid(z)
    o_ref[0] = x + _dot(z.astype(BF16), w2_ref[...])


def _conformer(x, g, w1, dww, dwb, lng, lnb, w2, *, tm):
    b, s, d = x.shape
    kern = functools.partial(_conformer_kernel, tm=tm)
    return pl.pallas_call(
        kern,
        grid=(b, s // tm),
        in_specs=[
            pl.BlockSpec((1, tm, d), lambda bi, i: (bi, i, 0)),
            _const_spec(g.shape), _const_spec(w1.shape), _const_spec(dww.shape),
            _const_spec(dwb.shape), _const_spec(lng.shape), _const_spec(lnb.shape),
            _const_spec(w2.shape),
        ],
        out_specs=pl.BlockSpec((1, tm, d), lambda bi, i: (bi, i, 0)),
        out_shape=jax.ShapeDtypeStruct((b, s, d), F32),
        scratch_shapes=[pltpu.VMEM((CONV_HALO + tm, D_MODEL), F32), pltpu.VMEM((tm, D_MODEL), F32)],
        compiler_params=pltpu.CompilerParams(
            dimension_semantics=("arbitrary", "arbitrary"), vmem_limit_bytes=VMEM_LIMIT),
        name="conformer",
    )(x, g, w1, dww, dwb, lng, lnb, w2)


def _conv_ffn_kernel(x_ref, g_ref, wup_ref, cw_ref, cb_ref, wdown_ref, o_ref, act_ref, carry_ref,
                     *, tm, chunk):
    @pl.when(pl.program_id(1) == 0)
    def _():
        carry_ref[...] = jnp.zeros_like(carry_ref)

    x = x_ref[0]
    hb = _rms_norm(x, g_ref[...]).astype(BF16)

    def conv_cols(col):
        cols = slice(col, col + chunk)
        up = _dot(hb, wup_ref[:, cols])
        ext = jnp.concatenate([carry_ref[:, cols], up], axis=0)
        carry_ref[:, cols] = up[tm - SUBLANES:, :]
        y = cb_ref[:, cols] + cw_ref[2:3, cols] * up
        y = y + cw_ref[1:2, cols] * pltpu.roll(ext, 1, axis=0)[SUBLANES:]
        y = y + cw_ref[0:1, cols] * pltpu.roll(ext, 2, axis=0)[SUBLANES:]
        return y

    for j in range(D_FF // chunk):
        gate = conv_cols(j * chunk)
        val = conv_cols(D_FF + j * chunk)
        act_ref[:, j * chunk:(j + 1) * chunk] = (gate * jax.nn.sigmoid(gate) * val).astype(BF16)

    o_ref[0] = x + _dot(act_ref[...], wdown_ref[...])


def _conv_ffn(x, g, wup, cw, cb, wdown, *, tm, chunk):
    b, s, d = x.shape
    kern = functools.partial(_conv_ffn_kernel, tm=tm, chunk=chunk)
    return pl.pallas_call(
        kern,
        grid=(b, s // tm),
        in_specs=[
            pl.BlockSpec((1, tm, d), lambda bi, i: (bi, i, 0)),
            _const_spec(g.shape), _const_spec(wup.shape), _const_spec(cw.shape),
            _const_spec(cb.shape), _const_spec(wdown.shape),
        ],
        out_specs=pl.BlockSpec((1, tm, d), lambda bi, i: (bi, i, 0)),
        out_shape=jax.ShapeDtypeStruct((b, s, d), F32),
        scratch_shapes=[pltpu.VMEM((tm, D_FF), BF16), pltpu.VMEM((SUBLANES, 2 * D_FF), F32)],
        compiler_params=pltpu.CompilerParams(
            dimension_semantics=("arbitrary", "arbitrary"), vmem_limit_bytes=VMEM_LIMIT),
        name="conv_ffn",
    )(x, g, wup, cw, cb, wdown)


def _pad_rows(a, rows):
    return jnp.pad(a, ((0, rows - a.shape[0]), (0, 0)))


def kernel(x, even_norm_g, even_w_in, even_b_f, even_q_norm_g, even_k_norm_g, even_w_pool,
           even_pool_scale, even_w_out, odd_norm_g, odd_w_pw1, odd_dw_w, odd_dw_b, odd_ln_g,
           odd_ln_b, odd_w_pw2, ffn_norm_g, ffn_w_up, ffn_conv_w, ffn_conv_b, ffn_w_down):
    depth = ffn_norm_g.shape[0]
    seq = x.shape[1]
    tm = min(512, seq)
    blk = min(2048, seq)
    tril = jnp.tril(jnp.ones((tm, tm), BF16))
    place = _gate_placement()
    row = lambda v: v.reshape(1, -1).astype(F32)

    for layer in range(depth):
        i = layer // 2
        if layer % 2 == 0:
            w_in = even_w_in[i]
            o = 2 * ATTN_DIM
            wqk = w_in[:, :o].astype(BF16)
            wvt = w_in[:, o:o + ATTN_DIM].T.astype(BF16)
            o += ATTN_DIM
            wf = jnp.pad(jnp.tile(w_in[:, o:o + N_HEADS], (1, 3)),
                         ((0, 0), (0, LANES - 3 * N_HEADS))).astype(BF16)
            bf = jnp.pad(jnp.tile(even_b_f[i], 3), (0, LANES - 3 * N_HEADS)).reshape(1, LANES).astype(F32)
            o += N_HEADS
            wu = w_in[:, o:o + POOL_DIM].astype(BF16)
            qg = row(jnp.tile(even_q_norm_g[i], 2))
            kg = row(jnp.tile(even_k_norm_g[i], 2))
            q, k, vt, u = _even_in(x, row(even_norm_g[i]), wqk, wvt, wu, wf, bf, qg, kg, tril, place,
                                   tm=tm)
            a = _fox_attention(q, k, vt, bq=blk)
            x = _even_out(a, u, x, even_w_pool[i].astype(BF16), row(even_pool_scale[i]),
                          even_w_out[i].astype(BF16), tm=tm)
        else:
            x = _conformer(x, row(odd_norm_g[i]), odd_w_pw1[i].astype(BF16),
                           _pad_rows(odd_dw_w[i].astype(F32), CONV_HALO), row(odd_dw_b[i]),
                           row(odd_ln_g[i]), row(odd_ln_b[i]), odd_w_pw2[i].astype(BF16), tm=tm)
        x = _conv_ffn(x, row(ffn_norm_g[layer]), ffn_w_up[layer].astype(BF16),
                      _pad_rows(ffn_conv_w[layer].astype(F32), SUBLANES), row(ffn_conv_b[layer]),
                      ffn_w_down[layer].astype(BF16), tm=tm, chunk=MXU_WIDTH)
    return x
```

```python
import functools
import math

import jax
import jax.numpy as jnp
import numpy as np
from jax import lax
from jax.experimental import pallas as pl
from jax.experimental.pallas import tpu as pltpu

D_MODEL = 1024
N_HEADS = 8
HEAD_DIM = 64
ATTN_DIM = N_HEADS * HEAD_DIM
POOL_WINDOWS = (2, 4, 8, 16)
POOL_GROUP_DIM = 128
POOL_DIM = len(POOL_WINDOWS) * POOL_GROUP_DIM
CONV_WIDTH = 31
D_FF = 2816
FFN_CONV_WIDTH = 3
EPS = 1e-6

LANES = 128
SUBLANES = 8
MXU_WIDTH = 256
POOL_HALO = 16
CONV_HALO = 32
NEG_BIG = -1e30
LOG2E = math.log2(math.e)
BF16_SUBLANES = 16
VT_ROWS = HEAD_DIM + BF16_SUBLANES
KEY_SPLIT = 8
VMEM_LIMIT = 56 * 1024 * 1024

F32 = jnp.float32
BF16 = jnp.bfloat16


def _rms_norm(x, g):
    ms = jnp.mean(x * x, axis=-1, keepdims=True)
    return x * lax.rsqrt(ms + EPS) * g


def _dot(a, b):
    return jnp.dot(a, b, preferred_element_type=F32)


def _dot_nt(a, b):
    return lax.dot_general(a, b, (((1,), (1,)), ((), ())), preferred_element_type=F32)


def _split3(c):
    hi = c.astype(BF16).astype(F32)
    r = c - hi
    mid = r.astype(BF16).astype(F32)
    lo = (r - mid).astype(BF16).astype(F32)
    return hi, mid, lo


def _const_spec(shape):
    zeros = (0,) * len(shape)
    return pl.BlockSpec(shape, lambda *_: zeros, pipeline_mode=pl.Buffered(1))


def _even_in_kernel(x_ref, g_ref, wqk_ref, wvt_ref, wu_ref, wf_ref, bf_ref, qg_ref, kg_ref,
                    tril_ref, place_ref, q_ref, k_ref, vt_ref, u_ref, carry_ref, *, tm):
    @pl.when(pl.program_id(1) == 0)
    def _():
        carry_ref[...] = jnp.zeros_like(carry_ref)

    hb = _rms_norm(x_ref[0], g_ref[...]).astype(BF16)
    qk = _dot(hb, wqk_ref[...])
    vt = _dot_nt(wvt_ref[...], hb).astype(BF16)
    for head in range(N_HEADS):
        vt_ref[0, head * VT_ROWS:head * VT_ROWS + HEAD_DIM] = vt[head * HEAD_DIM:(head + 1) * HEAD_DIM]
        vt_ref[0, head * VT_ROWS + HEAD_DIM:(head + 1) * VT_ROWS] = jnp.ones(
            (VT_ROWS - HEAD_DIM, tm), BF16)
    u_ref[0] = _dot(hb, wu_ref[...])

    fl = _dot(hb, wf_ref[...]) + bf_ref[...]
    logf = jnp.minimum(fl, 0.0) - jnp.log1p(jnp.exp(-jnp.abs(fl)))
    tril = tril_ref[...]
    c = carry_ref[0:1, :]
    for term in _split3(logf):
        c = c + _dot(tril, term.astype(BF16))
    carry_ref[0:1, :] = c[tm - 1:tm, :]
    c = c * LOG2E

    lane = lax.broadcasted_iota(jnp.int32, (tm, LANES), 1)
    hi, mid, lo = _split3(c)
    c3 = jnp.where(lane < N_HEADS, hi, jnp.where(lane < 2 * N_HEADS, mid,
         jnp.where(lane < 3 * N_HEADS, lo, jnp.where(lane == 3 * N_HEADS, 1.0, 0.0))))
    aug_qk = _dot(c3.astype(BF16), place_ref[...])

    low = lane < HEAD_DIM
    inv_dh = 1.0 / HEAD_DIM
    scale = LOG2E / math.sqrt(HEAD_DIM)
    for pair in range(N_HEADS // 2):
        for which, (gain_ref, out_ref) in enumerate(((qg_ref, q_ref), (kg_ref, k_ref))):
            col = which * ATTN_DIM + pair * LANES
            slab = qk[:, col:col + LANES]
            sq = slab * slab
            ss_lo = jnp.sum(jnp.where(low, sq, 0.0), axis=-1, keepdims=True)
            ss_hi = jnp.sum(jnp.where(low, 0.0, sq), axis=-1, keepdims=True)
            inv = jnp.where(low, lax.rsqrt(ss_lo * inv_dh + EPS), lax.rsqrt(ss_hi * inv_dh + EPS))
            normed = slab * inv * gain_ref[...]
            if which == 0:
                normed = normed * scale
            aug = aug_qk[:, col:col + LANES]
            out_ref[0, 2 * pair] = jnp.where(low, normed, aug).astype(BF16)
            out_ref[0, 2 * pair + 1] = jnp.where(low, aug, normed).astype(BF16)


def _gate_placement():
    place = np.zeros((LANES, 2 * ATTN_DIM), np.float32)
    for head in range(N_HEADS):
        base = (head // 2) * LANES + (HEAD_DIM if head % 2 == 0 else 0)
        for term in range(3):
            place[term * N_HEADS + head, base + term] = 1.0
            place[3 * N_HEADS, base + 3 + term] = 1.0
            place[3 * N_HEADS, ATTN_DIM + base + term] = 1.0
            place[term * N_HEADS + head, ATTN_DIM + base + 3 + term] = -1.0
    return jnp.asarray(place, BF16)


def _even_in(x, g, wqk, wvt, wu, wf, bf, qg, kg, tril, place, *, tm):
    b, s, d = x.shape
    kern = functools.partial(_even_in_kernel, tm=tm)
    return pl.pallas_call(
        kern,
        grid=(b, s // tm),
        in_specs=[
            pl.BlockSpec((1, tm, d), lambda bi, i: (bi, i, 0)),
            _const_spec(g.shape), _const_spec(wqk.shape), _const_spec(wvt.shape),
            _const_spec(wu.shape), _const_spec(wf.shape), _const_spec(bf.shape),
            _const_spec(qg.shape), _const_spec(kg.shape), _const_spec(tril.shape),
            _const_spec(place.shape),
        ],
        out_specs=[
            pl.BlockSpec((1, N_HEADS, tm, LANES), lambda bi, i: (bi, 0, i, 0)),
            pl.BlockSpec((1, N_HEADS, tm, LANES), lambda bi, i: (bi, 0, i, 0)),
            pl.BlockSpec((1, N_HEADS * VT_ROWS, tm), lambda bi, i: (bi, 0, i)),
            pl.BlockSpec((1, tm, POOL_DIM), lambda bi, i: (bi, i, 0)),
        ],
        out_shape=[
            jax.ShapeDtypeStruct((b, N_HEADS, s, LANES), BF16),
            jax.ShapeDtypeStruct((b, N_HEADS, s, LANES), BF16),
            jax.ShapeDtypeStruct((b, N_HEADS * VT_ROWS, s), BF16),
            jax.ShapeDtypeStruct((b, s, POOL_DIM), F32),
        ],
        scratch_shapes=[pltpu.VMEM((SUBLANES, LANES), F32)],
        compiler_params=pltpu.CompilerParams(
            dimension_semantics=("arbitrary", "arbitrary"), vmem_limit_bytes=VMEM_LIMIT),
        name="even_in",
    )(x, g, wqk, wvt, wu, wf, bf, qg, kg, tril, place)


def _attn_kernel(q_ref, k_ref, vt_ref, o_ref, sa_ref, sb_ref, m_ref, mb_ref, acc_ref, ot_ref, *, bq):
    bk = bq // KEY_SPLIT
    i = pl.program_id(2)
    chunks = [slice(c, c + MXU_WIDTH) for c in range(0, bq, MXU_WIDTH)]

    def diagonal_tile(d, cols):
        visible = min(bk, cols.stop - d * bk)
        shift = d * bk - cols.start
        return max(visible, 0), (shift if shift + visible - 1 > 0 else None)

    def one_head(hh, carry):
        def fill(s_ref, mb_row, blk_idx, cols, rows=bk, causal_shift=None):
            start = pl.multiple_of(blk_idx * bk, bk)
            s = _dot_nt(k_ref[0, hh, pl.ds(start, rows), :], q_ref[0, hh, cols, :])
            if causal_shift is not None:
                key = lax.broadcasted_iota(jnp.int32, s.shape, 0) + causal_shift
                s = jnp.where(key <= lax.broadcasted_iota(jnp.int32, s.shape, 1), s, NEG_BIG)
            s_ref[cols.start // MXU_WIDTH, 0:rows, :] = s
            mb_ref[mb_row:mb_row + 1, cols] = jnp.max(s, axis=0, keepdims=True)

        def drain(s_ref, mb_row, blk_idx, cols, rows=bk):
            m = m_ref[:, cols]
            m_new = jnp.maximum(m, mb_ref[mb_row:mb_row + 1, cols])
            m_ref[:, cols] = m_new
            p = jnp.exp2(s_ref[cols.start // MXU_WIDTH, 0:rows, :] - m_new).astype(BF16)
            start = pl.multiple_of(blk_idx * bk, bk)
            vb = vt_ref[0, pl.ds(pl.multiple_of(hh * VT_ROWS, BF16_SUBLANES), VT_ROWS), pl.ds(start, rows)]
            acc_ref[:, cols] = jnp.exp2(m - m_new) * acc_ref[:, cols] + _dot(vb, p)

        def fill_diagonal(d, s_ref, mb_row, cols):
            rows, shift = diagonal_tile(d, cols)
            if rows:
                fill(s_ref, mb_row, KEY_SPLIT * i + d, cols, rows=rows, causal_shift=shift)

        def drain_diagonal(d, s_ref, mb_row, cols):
            rows, _ = diagonal_tile(d, cols)
            if rows:
                drain(s_ref, mb_row, KEY_SPLIT * i + d, cols, rows=rows)

        def step(jj, last=False):
            for cols in chunks:
                fill(sb_ref, 1, 2 * jj + 1, cols)
                drain(sa_ref, 0, 2 * jj, cols)
            for cols in chunks:
                if last:
                    fill_diagonal(0, sa_ref, 0, cols)
                else:
                    fill(sa_ref, 0, 2 * jj + 2, cols)
                drain(sb_ref, 1, 2 * jj + 1, cols)

        def body(jj, carry):
            step(2 * jj)
            step(2 * jj + 1)
            return carry

        m_ref[...] = jnp.full((1, bq), NEG_BIG, F32)
        acc_ref[...] = jnp.zeros((VT_ROWS, bq), F32)

        @pl.when(i == 0)
        def _():
            for cols in chunks:
                fill_diagonal(0, sa_ref, 0, cols)

        @pl.when(i > 0)
        def _():
            for cols in chunks:
                fill(sa_ref, 0, 0, cols)

        steps = (KEY_SPLIT // 2) * i
        plain = jnp.maximum(steps - 1, 0)
        lax.fori_loop(0, plain // 2, body, 0)

        @pl.when(plain % 2 == 1)
        def _():
            step(plain - 1)

        @pl.when(i > 0)
        def _():
            step(steps - 1, last=True)

        bufs = ((sa_ref, 0), (sb_ref, 1))
        for d in range(1, KEY_SPLIT):
            for cols in chunks:
                fill_diagonal(d, *bufs[d % 2], cols)
                drain_diagonal(d - 1, *bufs[(d - 1) % 2], cols)
        for cols in chunks:
            drain_diagonal(KEY_SPLIT - 1, *bufs[(KEY_SPLIT - 1) % 2], cols)
        ot_ref[pl.ds(pl.multiple_of(hh * HEAD_DIM, HEAD_DIM), HEAD_DIM), :] = (
            acc_ref[:HEAD_DIM] / acc_ref[HEAD_DIM:HEAD_DIM + 1])
        return carry

    lax.fori_loop(0, 2, one_head, 0)
    o_ref[0] = ot_ref[...].T.astype(BF16)


def _fox_attention(q, k, vt, *, bq):
    b, nh, s, _ = q.shape
    kern = functools.partial(_attn_kernel, bq=bq)
    return pl.pallas_call(
        kern,
        grid=(b, nh // 2, s // bq),
        in_specs=[
            pl.BlockSpec((1, 2, bq, LANES), lambda bi, p, i: (bi, p, i, 0)),
            pl.BlockSpec((1, 2, s, LANES), lambda bi, p, i: (bi, p, 0, 0)),
            pl.BlockSpec((1, 2 * VT_ROWS, s), lambda bi, p, i: (bi, p, 0)),
        ],
        out_specs=pl.BlockSpec((1, bq, 2 * HEAD_DIM), lambda bi, p, i: (bi, i, p)),
        out_shape=jax.ShapeDtypeStruct((b, s, ATTN_DIM), BF16),
        scratch_shapes=[
            pltpu.VMEM((bq // MXU_WIDTH, bq // KEY_SPLIT, MXU_WIDTH), F32),
            pltpu.VMEM((bq // MXU_WIDTH, bq // KEY_SPLIT, MXU_WIDTH), F32),
            pltpu.VMEM((1, bq), F32), pltpu.VMEM((2, bq), F32),
            pltpu.VMEM((VT_ROWS, bq), F32),
            pltpu.VMEM((2 * HEAD_DIM, bq), F32)],
        compiler_params=pltpu.CompilerParams(
            dimension_semantics=("arbitrary", "arbitrary", "arbitrary"),
            vmem_limit_bytes=VMEM_LIMIT),
        name="fox_attention",
    )(q, k, vt)


def _even_out_kernel(a_ref, u_ref, x_ref, wpool_ref, pscale_ref, wout_ref, o_ref, carry_ref, *, tm):
    i = pl.program_id(1)

    @pl.when(i == 0)
    def _():
        carry_ref[...] = jnp.zeros_like(carry_ref)

    u = u_ref[0]
    pos = (i * tm + 1 + lax.broadcasted_iota(jnp.int32, (tm, 1), 0)).astype(F32)
    parts = [a_ref[0]]
    for g, w in enumerate(POOL_WINDOWS):
        cols = slice(g * POOL_GROUP_DIM, (g + 1) * POOL_GROUP_DIM)
        ug = u[:, cols]
        run = jnp.concatenate([carry_ref[:, cols], ug], axis=0)
        sh = 1
        while sh < w:
            run = run + pltpu.roll(run, sh, axis=0)
            sh *= 2
        pooled = run[POOL_HALO:] / jnp.minimum(pos, float(w))
        mixed = (pooled - ug).astype(BF16)
        parts.append((_dot(mixed, wpool_ref[g]) * pscale_ref[:, cols]).astype(BF16))
    carry_ref[...] = u[tm - POOL_HALO:, :]
    cat = jnp.concatenate(parts, axis=1)
    o_ref[0] = x_ref[0] + _dot(cat, wout_ref[...])


def _even_out(a, u, x, wpool, pscale, wout, *, tm):
    b, s, d = x.shape
    kern = functools.partial(_even_out_kernel, tm=tm)
    return pl.pallas_call(
        kern,
        grid=(b, s // tm),
        in_specs=[
            pl.BlockSpec((1, tm, ATTN_DIM), lambda bi, i: (bi, i, 0)),
            pl.BlockSpec((1, tm, POOL_DIM), lambda bi, i: (bi, i, 0)),
            pl.BlockSpec((1, tm, d), lambda bi, i: (bi, i, 0)),
            _const_spec(wpool.shape), _const_spec(pscale.shape), _const_spec(wout.shape),
        ],
        out_specs=pl.BlockSpec((1, tm, d), lambda bi, i: (bi, i, 0)),
        out_shape=jax.ShapeDtypeStruct((b, s, d), F32),
        scratch_shapes=[pltpu.VMEM((POOL_HALO, POOL_DIM), F32)],
        compiler_params=pltpu.CompilerParams(
            dimension_semantics=("arbitrary", "arbitrary"), vmem_limit_bytes=VMEM_LIMIT),
        name="even_out",
    )(a, u, x, wpool, pscale, wout)


def _conformer_kernel(x_ref, g_ref, w1_ref, dww_ref, dwb_ref, lng_ref, lnb_ref, w2_ref, o_ref,
                      ext_ref, y_ref, *, tm):
    i = pl.program_id(1)

    @pl.when(i == 0)
    def _():
        ext_ref[0:CONV_HALO, :] = jnp.zeros((CONV_HALO, D_MODEL), F32)

    @pl.when(i > 0)
    def _():
        ext_ref[0:CONV_HALO, :] = ext_ref[tm:tm + CONV_HALO, :]

    x = x_ref[0]
    hb = _rms_norm(x, g_ref[...]).astype(BF16)

    def conv_chunk(col):
        cols = slice(col, col + LANES)
        e = ext_ref[:, cols]
        acc = jnp.broadcast_to(dwb_ref[:, cols], (tm, LANES))
        for r in range(SUBLANES):
            er = e if r == 0 else pltpu.roll(e, r, axis=0)
            for a in range(CONV_HALO // SUBLANES):
                shift = SUBLANES * a + r
                if shift >= CONV_WIDTH:
                    continue
                tap = CONV_WIDTH - 1 - shift
                lo = CONV_HALO - SUBLANES * a
                acc = acc + dww_ref[tap:tap + 1, cols] * er[lo:lo + tm]
        y_ref[:, cols] = acc

    for col in range(0, D_MODEL, MXU_WIDTH):
        a = _dot(hb, w1_ref[:, col:col + MXU_WIDTH])
        gate = _dot(hb, w1_ref[:, D_MODEL + col:D_MODEL + col + MXU_WIDTH])
        ext_ref[CONV_HALO:, col:col + MXU_WIDTH] = a * jax.nn.sigmoid(gate)
        for sub in range(0, MXU_WIDTH, LANES):
            conv_chunk(col + sub)

    y = y_ref[...]
    mu = jnp.mean(y, axis=-1, keepdims=True)
    yc = y - mu
    var = jnp.mean(yc * yc, axis=-1, keepdims=True)
    z = yc * lax.rsqrt(var + EPS) * lng_ref[...] + lnb_ref[...]
    z = z * jax.nn.sigmoid(z)
    o_ref[0] = x + _dot(z.astype(BF16), w2_ref[...])


def _conformer(x, g, w1, dww, dwb, lng, lnb, w2, *, tm):
    b, s, d = x.shape
    kern = functools.partial(_conformer_kernel, tm=tm)
    return pl.pallas_call(
        kern,
        grid=(b, s // tm),
        in_specs=[
            pl.BlockSpec((1, tm, d), lambda bi, i: (bi, i, 0)),
            _const_spec(g.shape), _const_spec(w1.shape), _const_spec(dww.shape),
            _const_spec(dwb.shape), _const_spec(lng.shape), _const_spec(lnb.shape),
            _const_spec(w2.shape),
        ],
        out_specs=pl.BlockSpec((1, tm, d), lambda bi, i: (bi, i, 0)),
        out_shape=jax.ShapeDtypeStruct((b, s, d), F32),
        scratch_shapes=[pltpu.VMEM((CONV_HALO + tm, D_MODEL), F32), pltpu.VMEM((tm, D_MODEL), F32)],
        compiler_params=pltpu.CompilerParams(
            dimension_semantics=("arbitrary", "arbitrary"), vmem_limit_bytes=VMEM_LIMIT),
        name="conformer",
    )(x, g, w1, dww, dwb, lng, lnb, w2)


def _conv_ffn_kernel(x_ref, g_ref, wup_ref, cw_ref, cb_ref, wdown_ref, o_ref, act_ref, carry_ref,
                     *, tm, chunk):
    @pl.when(pl.program_id(1) == 0)
    def _():
        carry_ref[...] = jnp.zeros_like(carry_ref)

    x = x_ref[0]
    hb = _rms_norm(x, g_ref[...]).astype(BF16)

    def conv_cols(col):
        cols = slice(col, col + chunk)
        up = _dot(hb, wup_ref[:, cols])
        ext = jnp.concatenate([carry_ref[:, cols], up], axis=0)
        carry_ref[:, cols] = up[tm - SUBLANES:, :]
        y = cb_ref[:, cols] + cw_ref[2:3, cols] * up
        y = y + cw_ref[1:2, cols] * pltpu.roll(ext, 1, axis=0)[SUBLANES:]
        y = y + cw_ref[0:1, cols] * pltpu.roll(ext, 2, axis=0)[SUBLANES:]
        return y

    for j in range(D_FF // chunk):
        gate = conv_cols(j * chunk)
        val = conv_cols(D_FF + j * chunk)
        act_ref[:, j * chunk:(j + 1) * chunk] = (gate * jax.nn.sigmoid(gate) * val).astype(BF16)

    o_ref[0] = x + _dot(act_ref[...], wdown_ref[...])


def _conv_ffn(x, g, wup, cw, cb, wdown, *, tm, chunk):
    b, s, d = x.shape
    kern = functools.partial(_conv_ffn_kernel, tm=tm, chunk=chunk)
    return pl.pallas_call(
        kern,
        grid=(b, s // tm),
        in_specs=[
            pl.BlockSpec((1, tm, d), lambda bi, i: (bi, i, 0)),
            _const_spec(g.shape), _const_spec(wup.shape), _const_spec(cw.shape),
            _const_spec(cb.shape), _const_spec(wdown.shape),
        ],
        out_specs=pl.BlockSpec((1, tm, d), lambda bi, i: (bi, i, 0)),
        out_shape=jax.ShapeDtypeStruct((b, s, d), F32),
        scratch_shapes=[pltpu.VMEM((tm, D_FF), BF16), pltpu.VMEM((SUBLANES, 2 * D_FF), F32)],
        compiler_params=pltpu.CompilerParams(
            dimension_semantics=("arbitrary", "arbitrary"), vmem_limit_bytes=VMEM_LIMIT),
        name="conv_ffn",
    )(x, g, wup, cw, cb, wdown)


def _pad_rows(a, rows):
    return jnp.pad(a, ((0, rows - a.shape[0]), (0, 0)))


def kernel(x, even_norm_g, even_w_in, even_b_f, even_q_norm_g, even_k_norm_g, even_w_pool,
           even_pool_scale, even_w_out, odd_norm_g, odd_w_pw1, odd_dw_w, odd_dw_b, odd_ln_g,
           odd_ln_b, odd_w_pw2, ffn_norm_g, ffn_w_up, ffn_conv_w, ffn_conv_b, ffn_w_down):
    depth = ffn_norm_g.shape[0]
    seq = x.shape[1]
    tm = min(512, seq)
    blk = min(4096, seq)
    tril = jnp.tril(jnp.ones((tm, tm), BF16))
    place = _gate_placement()
    row = lambda v: v.reshape(1, -1).astype(F32)

    for layer in range(depth):
        i = layer // 2
        if layer % 2 == 0:
            w_in = even_w_in[i]
            o = 2 * ATTN_DIM
            wqk = w_in[:, :o].astype(BF16)
            wvt = w_in[:, o:o + ATTN_DIM].T.astype(BF16)
            o += ATTN_DIM
            wf = jnp.pad(jnp.tile(w_in[:, o:o + N_HEADS], (1, 3)),
                         ((0, 0), (0, LANES - 3 * N_HEADS))).astype(BF16)
            bf = jnp.pad(jnp.tile(even_b_f[i], 3), (0, LANES - 3 * N_HEADS)).reshape(1, LANES).astype(F32)
            o += N_HEADS
            wu = w_in[:, o:o + POOL_DIM].astype(BF16)
            qg = row(jnp.tile(even_q_norm_g[i], 2))
            kg = row(jnp.tile(even_k_norm_g[i], 2))
            q, k, vt, u = _even_in(x, row(even_norm_g[i]), wqk, wvt, wu, wf, bf, qg, kg, tril, place,
                                   tm=tm)
            a = _fox_attention(q, k, vt, bq=blk)
            x = _even_out(a, u, x, even_w_pool[i].astype(BF16), row(even_pool_scale[i]),
                          even_w_out[i].astype(BF16), tm=tm)
        else:
            x = _conformer(x, row(odd_norm_g[i]), odd_w_pw1[i].astype(BF16),
                           _pad_rows(odd_dw_w[i].astype(F32), CONV_HALO), row(odd_dw_b[i]),
                           row(odd_ln_g[i]), row(odd_ln_b[i]), odd_w_pw2[i].astype(BF16), tm=tm)
        x = _conv_ffn(x, row(ffn_norm_g[layer]), ffn_w_up[layer].astype(BF16),
                      _pad_rows(ffn_conv_w[layer].astype(F32), SUBLANES), row(ffn_conv_b[layer]),
                      ffn_w_down[layer].astype(BF16), tm=tm, chunk=MXU_WIDTH)
    return x
```

```python
import functools
import math

import jax
import jax.numpy as jnp
import numpy as np
from jax import lax
from jax.experimental import pallas as pl
from jax.experimental.pallas import tpu as pltpu

D_MODEL = 1024
N_HEADS = 8
HEAD_DIM = 64
ATTN_DIM = N_HEADS * HEAD_DIM
POOL_WINDOWS = (2, 4, 8, 16)
POOL_GROUP_DIM = 128
POOL_DIM = len(POOL_WINDOWS) * POOL_GROUP_DIM
CONV_WIDTH = 31
D_FF = 2816
FFN_CONV_WIDTH = 3
EPS = 1e-6

LANES = 128
SUBLANES = 8
MXU_WIDTH = 256
POOL_HALO = 16
CONV_HALO = 32
NEG_BIG = -1e30
LOG2E = math.log2(math.e)
BF16_SUBLANES = 16
VT_ROWS = HEAD_DIM + BF16_SUBLANES
KEY_SPLIT = 8
VMEM_LIMIT = 56 * 1024 * 1024

F32 = jnp.float32
BF16 = jnp.bfloat16


def _rms_norm(x, g):
    ms = jnp.mean(x * x, axis=-1, keepdims=True)
    return x * lax.rsqrt(ms + EPS) * g


def _dot(a, b):
    return jnp.dot(a, b, preferred_element_type=F32)


def _dot_nt(a, b):
    return lax.dot_general(a, b, (((1,), (1,)), ((), ())), preferred_element_type=F32)


def _split3(c):
    hi = c.astype(BF16).astype(F32)
    r = c - hi
    mid = r.astype(BF16).astype(F32)
    lo = (r - mid).astype(BF16).astype(F32)
    return hi, mid, lo


def _const_spec(shape):
    zeros = (0,) * len(shape)
    return pl.BlockSpec(shape, lambda *_: zeros, pipeline_mode=pl.Buffered(1))


def _even_in_kernel(x_ref, g_ref, wqk_ref, wvt_ref, wu_ref, wf_ref, bf_ref, qg_ref, kg_ref,
                    tril_ref, place_ref, q_ref, k_ref, vt_ref, u_ref, carry_ref, *, tm):
    @pl.when(pl.program_id(1) == 0)
    def _():
        carry_ref[...] = jnp.zeros_like(carry_ref)

    hb = _rms_norm(x_ref[0], g_ref[...]).astype(BF16)
    qk = _dot(hb, wqk_ref[...])
    vt = _dot_nt(wvt_ref[...], hb).astype(BF16)
    for head in range(N_HEADS):
        vt_ref[0, head * VT_ROWS:head * VT_ROWS + HEAD_DIM] = vt[head * HEAD_DIM:(head + 1) * HEAD_DIM]
        vt_ref[0, head * VT_ROWS + HEAD_DIM:(head + 1) * VT_ROWS] = jnp.ones(
            (VT_ROWS - HEAD_DIM, tm), BF16)
    u_ref[0] = _dot(hb, wu_ref[...])

    fl = _dot(hb, wf_ref[...]) + bf_ref[...]
    logf = jnp.minimum(fl, 0.0) - jnp.log1p(jnp.exp(-jnp.abs(fl)))
    tril = tril_ref[...]
    c = carry_ref[0:1, :]
    for term in _split3(logf):
        c = c + _dot(tril, term.astype(BF16))
    carry_ref[0:1, :] = c[tm - 1:tm, :]
    c = c * LOG2E

    lane = lax.broadcasted_iota(jnp.int32, (tm, LANES), 1)
    hi, mid, lo = _split3(c)
    c3 = jnp.where(lane < N_HEADS, hi, jnp.where(lane < 2 * N_HEADS, mid,
         jnp.where(lane < 3 * N_HEADS, lo, jnp.where(lane == 3 * N_HEADS, 1.0, 0.0))))
    aug_qk = _dot(c3.astype(BF16), place_ref[...])

    low = lane < HEAD_DIM
    inv_dh = 1.0 / HEAD_DIM
    scale = LOG2E / math.sqrt(HEAD_DIM)
    for pair in range(N_HEADS // 2):
        for which, (gain_ref, out_ref) in enumerate(((qg_ref, q_ref), (kg_ref, k_ref))):
            col = which * ATTN_DIM + pair * LANES
            slab = qk[:, col:col + LANES]
            sq = slab * slab
            ss_lo = jnp.sum(jnp.where(low, sq, 0.0), axis=-1, keepdims=True)
            ss_hi = jnp.sum(jnp.where(low, 0.0, sq), axis=-1, keepdims=True)
            inv = jnp.where(low, lax.rsqrt(ss_lo * inv_dh + EPS), lax.rsqrt(ss_hi * inv_dh + EPS))
            normed = slab * inv * gain_ref[...]
            if which == 0:
                normed = normed * scale
            aug = aug_qk[:, col:col + LANES]
            out_ref[0, 2 * pair] = jnp.where(low, normed, aug).astype(BF16)
            out_ref[0, 2 * pair + 1] = jnp.where(low, aug, normed).astype(BF16)


def _gate_placement():
    place = np.zeros((LANES, 2 * ATTN_DIM), np.float32)
    for head in range(N_HEADS):
        base = (head // 2) * LANES + (HEAD_DIM if head % 2 == 0 else 0)
        for term in range(3):
            place[term * N_HEADS + head, base + term] = 1.0
            place[3 * N_HEADS, base + 3 + term] = 1.0
            place[3 * N_HEADS, ATTN_DIM + base + term] = 1.0
            place[term * N_HEADS + head, ATTN_DIM + base + 3 + term] = -1.0
    return jnp.asarray(place, BF16)


def _even_in(x, g, wqk, wvt, wu, wf, bf, qg, kg, tril, place, *, tm):
    b, s, d = x.shape
    kern = functools.partial(_even_in_kernel, tm=tm)
    return pl.pallas_call(
        kern,
        grid=(b, s // tm),
        in_specs=[
            pl.BlockSpec((1, tm, d), lambda bi, i: (bi, i, 0)),
            _const_spec(g.shape), _const_spec(wqk.shape), _const_spec(wvt.shape),
            _const_spec(wu.shape), _const_spec(wf.shape), _const_spec(bf.shape),
            _const_spec(qg.shape), _const_spec(kg.shape), _const_spec(tril.shape),
            _const_spec(place.shape),
        ],
        out_specs=[
            pl.BlockSpec((1, N_HEADS, tm, LANES), lambda bi, i: (bi, 0, i, 0)),
            pl.BlockSpec((1, N_HEADS, tm, LANES), lambda bi, i: (bi, 0, i, 0)),
            pl.BlockSpec((1, N_HEADS * VT_ROWS, tm), lambda bi, i: (bi, 0, i)),
            pl.BlockSpec((1, tm, POOL_DIM), lambda bi, i: (bi, i, 0)),
        ],
        out_shape=[
            jax.ShapeDtypeStruct((b, N_HEADS, s, LANES), BF16),
            jax.ShapeDtypeStruct((b, N_HEADS, s, LANES), BF16),
            jax.ShapeDtypeStruct((b, N_HEADS * VT_ROWS, s), BF16),
            jax.ShapeDtypeStruct((b, s, POOL_DIM), F32),
        ],
        scratch_shapes=[pltpu.VMEM((SUBLANES, LANES), F32)],
        compiler_params=pltpu.CompilerParams(
            dimension_semantics=("arbitrary", "arbitrary"), vmem_limit_bytes=VMEM_LIMIT),
        name="even_in",
    )(x, g, wqk, wvt, wu, wf, bf, qg, kg, tril, place)


def _attn_kernel(q_ref, k_ref, vt_ref, o_ref, sa_ref, sb_ref, m_ref, mb_ref, acc_ref, ot_ref, *, bq):
    assert KEY_SPLIT % 4 == 0
    bk = bq // KEY_SPLIT
    i = pl.program_id(2)
    chunks = [slice(c, c + MXU_WIDTH) for c in range(0, bq, MXU_WIDTH)]

    def diagonal_tile(d, cols):
        visible = min(bk, cols.stop - d * bk)
        shift = d * bk - cols.start
        return max(visible, 0), (shift if shift + visible - 1 > 0 else None)

    def one_head(hh, carry):
        def fill(s_ref, mb_row, blk_idx, cols, rows=bk, causal_shift=None):
            start = pl.multiple_of(blk_idx * bk, bk)
            s = _dot_nt(k_ref[0, hh, pl.ds(start, rows), :], q_ref[0, hh, cols, :])
            if causal_shift is not None:
                key = lax.broadcasted_iota(jnp.int32, s.shape, 0) + causal_shift
                s = jnp.where(key <= lax.broadcasted_iota(jnp.int32, s.shape, 1), s, NEG_BIG)
            s_ref[cols.start // MXU_WIDTH, 0:rows, :] = s
            mb_ref[mb_row:mb_row + 1, cols] = jnp.max(s, axis=0, keepdims=True)

        def drain(s_ref, mb_row, blk_idx, cols, rows=bk):
            m = m_ref[:, cols]
            m_new = jnp.maximum(m, mb_ref[mb_row:mb_row + 1, cols])
            m_ref[:, cols] = m_new
            p = jnp.exp2(s_ref[cols.start // MXU_WIDTH, 0:rows, :] - m_new).astype(BF16)
            start = pl.multiple_of(blk_idx * bk, bk)
            vb = vt_ref[0, pl.ds(pl.multiple_of(hh * VT_ROWS, BF16_SUBLANES), VT_ROWS), pl.ds(start, rows)]
            acc_ref[:, cols] = jnp.exp2(m - m_new) * acc_ref[:, cols] + _dot(vb, p)

        def fill_diagonal(d, s_ref, mb_row, cols):
            rows, shift = diagonal_tile(d, cols)
            if rows:
                fill(s_ref, mb_row, KEY_SPLIT * i + d, cols, rows=rows, causal_shift=shift)

        def drain_diagonal(d, s_ref, mb_row, cols):
            rows, _ = diagonal_tile(d, cols)
            if rows:
                drain(s_ref, mb_row, KEY_SPLIT * i + d, cols, rows=rows)

        def step(jj, last=False):
            for cols in chunks:
                fill(sb_ref, 1, 2 * jj + 1, cols)
                drain(sa_ref, 0, 2 * jj, cols)
            for cols in chunks:
                if last:
                    fill_diagonal(0, sa_ref, 0, cols)
                else:
                    fill(sa_ref, 0, 2 * jj + 2, cols)
                drain(sb_ref, 1, 2 * jj + 1, cols)

        def body(jj, carry):
            step(2 * jj)
            step(2 * jj + 1)
            return carry

        m_ref[...] = jnp.full((1, bq), NEG_BIG, F32)
        acc_ref[...] = jnp.zeros((VT_ROWS, bq), F32)

        @pl.when(i == 0)
        def _():
            for cols in chunks:
                fill_diagonal(0, sa_ref, 0, cols)

        @pl.when(i > 0)
        def _():
            for cols in chunks:
                fill(sa_ref, 0, 0, cols)

        steps = (KEY_SPLIT // 2) * i
        lax.fori_loop(0, jnp.maximum(steps // 2 - 1, 0), body, 0)

        @pl.when(i > 0)
        def _():
            step(steps - 2)
            step(steps - 1, last=True)

        bufs = ((sa_ref, 0), (sb_ref, 1))
        for d in range(1, KEY_SPLIT):
            for cols in chunks:
                fill_diagonal(d, *bufs[d % 2], cols)
                drain_diagonal(d - 1, *bufs[(d - 1) % 2], cols)
        for cols in chunks:
            drain_diagonal(KEY_SPLIT - 1, *bufs[(KEY_SPLIT - 1) % 2], cols)
        ot_ref[pl.ds(pl.multiple_of(hh * HEAD_DIM, HEAD_DIM), HEAD_DIM), :] = (
            acc_ref[:HEAD_DIM] / acc_ref[HEAD_DIM:HEAD_DIM + 1])
        return carry

    lax.fori_loop(0, 2, one_head, 0)
    o_ref[0] = ot_ref[...].T.astype(BF16)


def _fox_attention(q, k, vt, *, bq):
    b, nh, s, _ = q.shape
    kern = functools.partial(_attn_kernel, bq=bq)
    return pl.pallas_call(
        kern,
        grid=(b, nh // 2, s // bq),
        in_specs=[
            pl.BlockSpec((1, 2, bq, LANES), lambda bi, p, i: (bi, p, i, 0)),
            pl.BlockSpec((1, 2, s, LANES), lambda bi, p, i: (bi, p, 0, 0)),
            pl.BlockSpec((1, 2 * VT_ROWS, s), lambda bi, p, i: (bi, p, 0)),
        ],
        out_specs=pl.BlockSpec((1, bq, 2 * HEAD_DIM), lambda bi, p, i: (bi, i, p)),
        out_shape=jax.ShapeDtypeStruct((b, s, ATTN_DIM), BF16),
        scratch_shapes=[
            pltpu.VMEM((bq // MXU_WIDTH, bq // KEY_SPLIT, MXU_WIDTH), F32),
            pltpu.VMEM((bq // MXU_WIDTH, bq // KEY_SPLIT, MXU_WIDTH), F32),
            pltpu.VMEM((1, bq), F32), pltpu.VMEM((2, bq), F32),
            pltpu.VMEM((VT_ROWS, bq), F32),
            pltpu.VMEM((2 * HEAD_DIM, bq), F32)],
        compiler_params=pltpu.CompilerParams(
            dimension_semantics=("arbitrary", "arbitrary", "arbitrary"),
            vmem_limit_bytes=VMEM_LIMIT),
        name="fox_attention",
    )(q, k, vt)


def _even_out_kernel(a_ref, u_ref, x_ref, wpool_ref, pscale_ref, wout_ref, o_ref, carry_ref, *, tm):
    i = pl.program_id(1)

    @pl.when(i == 0)
    def _():
        carry_ref[...] = jnp.zeros_like(carry_ref)

    u = u_ref[0]
    pos = (i * tm + 1 + lax.broadcasted_iota(jnp.int32, (tm, 1), 0)).astype(F32)
    parts = [a_ref[0]]
    for g, w in enumerate(POOL_WINDOWS):
        cols = slice(g * POOL_GROUP_DIM, (g + 1) * POOL_GROUP_DIM)
        ug = u[:, cols]
        run = jnp.concatenate([carry_ref[:, cols], ug], axis=0)
        sh = 1
        while sh < w:
            run = run + pltpu.roll(run, sh, axis=0)
            sh *= 2
        pooled = run[POOL_HALO:] / jnp.minimum(pos, float(w))
        mixed = (pooled - ug).astype(BF16)
        parts.append((_dot(mixed, wpool_ref[g]) * pscale_ref[:, cols]).astype(BF16))
    carry_ref[...] = u[tm - POOL_HALO:, :]
    cat = jnp.concatenate(parts, axis=1)
    o_ref[0] = x_ref[0] + _dot(cat, wout_ref[...])


def _even_out(a, u, x, wpool, pscale, wout, *, tm):
    b, s, d = x.shape
    kern = functools.partial(_even_out_kernel, tm=tm)
    return pl.pallas_call(
        kern,
        grid=(b, s // tm),
        in_specs=[
            pl.BlockSpec((1, tm, ATTN_DIM), lambda bi, i: (bi, i, 0)),
            pl.BlockSpec((1, tm, POOL_DIM), lambda bi, i: (bi, i, 0)),
            pl.BlockSpec((1, tm, d), lambda bi, i: (bi, i, 0)),
            _const_spec(wpool.shape), _const_spec(pscale.shape), _const_spec(wout.shape),
        ],
        out_specs=pl.BlockSpec((1, tm, d), lambda bi, i: (bi, i, 0)),
        out_shape=jax.ShapeDtypeStruct((b, s, d), F32),
        scratch_shapes=[pltpu.VMEM((POOL_HALO, POOL_DIM), F32)],
        compiler_params=pltpu.CompilerParams(
            dimension_semantics=("arbitrary", "arbitrary"), vmem_limit_bytes=VMEM_LIMIT),
        name="even_out",
    )(a, u, x, wpool, pscale, wout)


def _conformer_kernel(x_ref, g_ref, w1_ref, dww_ref, dwb_ref, lng_ref, lnb_ref, w2_ref, o_ref,
                      ext_ref, y_ref, *, tm):
    i = pl.program_id(1)

    @pl.when(i == 0)
    def _():
        ext_ref[0:CONV_HALO, :] = jnp.zeros((CONV_HALO, D_MODEL), F32)

    @pl.when(i > 0)
    def _():
        ext_ref[0:CONV_HALO, :] = ext_ref[tm:tm + CONV_HALO, :]

    x = x_ref[0]
    hb = _rms_norm(x, g_ref[...]).astype(BF16)

    def conv_chunk(col):
        cols = slice(col, col + LANES)
        e = ext_ref[:, cols]
        acc = jnp.broadcast_to(dwb_ref[:, cols], (tm, LANES))
        for r in range(SUBLANES):
            er = e if r == 0 else pltpu.roll(e, r, axis=0)
            for a in range(CONV_HALO // SUBLANES):
                shift = SUBLANES * a + r
                if shift >= CONV_WIDTH:
                    continue
                tap = CONV_WIDTH - 1 - shift
                lo = CONV_HALO - SUBLANES * a
                acc = acc + dww_ref[tap:tap + 1, cols] * er[lo:lo + tm]
        y_ref[:, cols] = acc

    for col in range(0, D_MODEL, MXU_WIDTH):
        a = _dot(hb, w1_ref[:, col:col + MXU_WIDTH])
        gate = _dot(hb, w1_ref[:, D_MODEL + col:D_MODEL + col + MXU_WIDTH])
        ext_ref[CONV_HALO:, col:col + MXU_WIDTH] = a * jax.nn.sigmoid(gate)
        for sub in range(0, MXU_WIDTH, LANES):
            conv_chunk(col + sub)

    y = y_ref[...]
    mu = jnp.mean(y, axis=-1, keepdims=True)
    yc = y - mu
    var = jnp.mean(yc * yc, axis=-1, keepdims=True)
    z = yc * lax.rsqrt(var + EPS) * lng_ref[...] + lnb_ref[...]
    z = z * jax.nn.sigmoid(z)
    o_ref[0] = x + _dot(z.astype(BF16), w2_ref[...])


def _conformer(x, g, w1, dww, dwb, lng, lnb, w2, *, tm):
    b, s, d = x.shape
    kern = functools.partial(_conformer_kernel, tm=tm)
    return pl.pallas_call(
        kern,
        grid=(b, s // tm),
        in_specs=[
            pl.BlockSpec((1, tm, d), lambda bi, i: (bi, i, 0)),
            _const_spec(g.shape), _const_spec(w1.shape), _const_spec(dww.shape),
            _const_spec(dwb.shape), _const_spec(lng.shape), _const_spec(lnb.shape),
            _const_spec(w2.shape),
        ],
        out_specs=pl.BlockSpec((1, tm, d), lambda bi, i: (bi, i, 0)),
        out_shape=jax.ShapeDtypeStruct((b, s, d), F32),
        scratch_shapes=[pltpu.VMEM((CONV_HALO + tm, D_MODEL), F32), pltpu.VMEM((tm, D_MODEL), F32)],
        compiler_params=pltpu.CompilerParams(
            dimension_semantics=("arbitrary", "arbitrary"), vmem_limit_bytes=VMEM_LIMIT),
        name="conformer",
    )(x, g, w1, dww, dwb, lng, lnb, w2)


def _conv_ffn_kernel(x_ref, g_ref, wup_ref, cw_ref, cb_ref, wdown_ref, o_ref, act_ref, carry_ref,
                     *, tm, chunk):
    @pl.when(pl.program_id(1) == 0)
    def _():
        carry_ref[...] = jnp.zeros_like(carry_ref)

    x = x_ref[0]
    hb = _rms_norm(x, g_ref[...]).astype(BF16)

    def conv_cols(col):
        cols = slice(col, col + chunk)
        up = _dot(hb, wup_ref[:, cols])
        ext = jnp.concatenate([carry_ref[:, cols], up], axis=0)
        carry_ref[:, cols] = up[tm - SUBLANES:, :]
        y = cb_ref[:, cols] + cw_ref[2:3, cols] * up
        y = y + cw_ref[1:2, cols] * pltpu.roll(ext, 1, axis=0)[SUBLANES:]
        y = y + cw_ref[0:1, cols] * pltpu.roll(ext, 2, axis=0)[SUBLANES:]
        return y

    for j in range(D_FF // chunk):
        gate = conv_cols(j * chunk)
        val = conv_cols(D_FF + j * chunk)
        act_ref[:, j * chunk:(j + 1) * chunk] = (gate * jax.nn.sigmoid(gate) * val).astype(BF16)

    o_ref[0] = x + _dot(act_ref[...], wdown_ref[...])


def _conv_ffn(x, g, wup, cw, cb, wdown, *, tm, chunk):
    b, s, d = x.shape
    kern = functools.partial(_conv_ffn_kernel, tm=tm, chunk=chunk)
    return pl.pallas_call(
        kern,
        grid=(b, s // tm),
        in_specs=[
            pl.BlockSpec((1, tm, d), lambda bi, i: (bi, i, 0)),
            _const_spec(g.shape), _const_spec(wup.shape), _const_spec(cw.shape),
            _const_spec(cb.shape), _const_spec(wdown.shape),
        ],
        out_specs=pl.BlockSpec((1, tm, d), lambda bi, i: (bi, i, 0)),
        out_shape=jax.ShapeDtypeStruct((b, s, d), F32),
        scratch_shapes=[pltpu.VMEM((tm, D_FF), BF16), pltpu.VMEM((SUBLANES, 2 * D_FF), F32)],
        compiler_params=pltpu.CompilerParams(
            dimension_semantics=("arbitrary", "arbitrary"), vmem_limit_bytes=VMEM_LIMIT),
        name="conv_ffn",
    )(x, g, wup, cw, cb, wdown)


def _pad_rows(a, rows):
    return jnp.pad(a, ((0, rows - a.shape[0]), (0, 0)))


def kernel(x, even_norm_g, even_w_in, even_b_f, even_q_norm_g, even_k_norm_g, even_w_pool,
           even_pool_scale, even_w_out, odd_norm_g, odd_w_pw1, odd_dw_w, odd_dw_b, odd_ln_g,
           odd_ln_b, odd_w_pw2, ffn_norm_g, ffn_w_up, ffn_conv_w, ffn_conv_b, ffn_w_down):
    depth = ffn_norm_g.shape[0]
    seq = x.shape[1]
    tm = min(512, seq)
    blk = min(4096, seq)
    tril = jnp.tril(jnp.ones((tm, tm), BF16))
    place = _gate_placement()
    row = lambda v: v.reshape(1, -1).astype(F32)

    for layer in range(depth):
        i = layer // 2
        if layer % 2 == 0:
            w_in = even_w_in[i]
            o = 2 * ATTN_DIM
            wqk = w_in[:, :o].astype(BF16)
            wvt = w_in[:, o:o + ATTN_DIM].T.astype(BF16)
            o += ATTN_DIM
            wf = jnp.pad(jnp.tile(w_in[:, o:o + N_HEADS], (1, 3)),
                         ((0, 0), (0, LANES - 3 * N_HEADS))).astype(BF16)
            bf = jnp.pad(jnp.tile(even_b_f[i], 3), (0, LANES - 3 * N_HEADS)).reshape(1, LANES).astype(F32)
            o += N_HEADS
            wu = w_in[:, o:o + POOL_DIM].astype(BF16)
            qg = row(jnp.tile(even_q_norm_g[i], 2))
            kg = row(jnp.tile(even_k_norm_g[i], 2))
            q, k, vt, u = _even_in(x, row(even_norm_g[i]), wqk, wvt, wu, wf, bf, qg, kg, tril, place,
                                   tm=tm)
            a = _fox_attention(q, k, vt, bq=blk)
            x = _even_out(a, u, x, even_w_pool[i].astype(BF16), row(even_pool_scale[i]),
                          even_w_out[i].astype(BF16), tm=tm)
        else:
            x = _conformer(x, row(odd_norm_g[i]), odd_w_pw1[i].astype(BF16),
                           _pad_rows(odd_dw_w[i].astype(F32), CONV_HALO), row(odd_dw_b[i]),
                           row(odd_ln_g[i]), row(odd_ln_b[i]), odd_w_pw2[i].astype(BF16), tm=tm)
        x = _conv_ffn(x, row(ffn_norm_g[layer]), ffn_w_up[layer].astype(BF16),
                      _pad_rows(ffn_conv_w[layer].astype(F32), SUBLANES), row(ffn_conv_b[layer]),
                      ffn_w_down[layer].astype(BF16), tm=tm, chunk=MXU_WIDTH)
    return x
```
